```python
import math
import jax, jax.numpy as jnp
from jax import lax
import numpy as np

D_MODEL = 4096
BATCH = 2
SEQ = 4096
DEPTH = 2

MIX_WIDTH = 2 * D_MODEL
SSD_WIDTH = MIX_WIDTH // 2
LRU_WIDTH = MIX_WIDTH // 4
MLSTM_WIDTH = MIX_WIDTH - SSD_WIDTH - LRU_WIDTH
SSD_HEAD_DIM = 64
SSD_HEADS = SSD_WIDTH // SSD_HEAD_DIM
SSD_GROUPS = 8
SSD_HEADS_PER_GROUP = SSD_HEADS // SSD_GROUPS
SSD_STATE = 128
SSD_CONV_DIM = SSD_WIDTH + 2 * SSD_GROUPS * SSD_STATE
LRU_BLOCKS = 16
LRU_BLOCK_DIM = LRU_WIDTH // LRU_BLOCKS
LRU_C = 8.0
MLSTM_HEADS = 8
MLSTM_HEAD_DIM = MLSTM_WIDTH // MLSTM_HEADS
CONV_WIDTH = 4
CHUNK = 128
DEEPNORM_ALPHA = (2.0 * DEPTH) ** 0.25
DEEPNORM_BETA = (8.0 * DEPTH) ** -0.25
IN_SIZES = (SSD_WIDTH, SSD_CONV_DIM, SSD_HEADS, LRU_WIDTH, LRU_WIDTH,
            MLSTM_WIDTH, MLSTM_WIDTH, MLSTM_WIDTH, 2 * MLSTM_HEADS)
N_IN = sum(IN_SIZES)

kernel_name = "hymba_style_ssd_rglru_mlstm_deepnorm"

F32 = jnp.float32


def _split_points():
    pts, acc = [], 0
    for s in IN_SIZES[:-1]:
        acc += s
        pts.append(acc)
    return pts


def causal_conv(x, w, b):
    c = x.shape[-1]
    y = lax.conv_general_dilated(x, w[:, None, :].astype(x.dtype), window_strides=(1,),
                                 padding=[(CONV_WIDTH - 1, 0)],
                                 dimension_numbers=("NWC", "WIO", "NWC"),
                                 feature_group_count=c)
    return y + b


def rms_norm_last(x, eps=1e-6):
    x = x.astype(F32)
    return x * lax.rsqrt(jnp.mean(x * x, axis=-1, keepdims=True) + eps)


def layer_norm(x, w, b, eps=1e-5):
    x = x.astype(F32)
    mu = jnp.mean(x, axis=-1, keepdims=True)
    xc = x - mu
    var = jnp.mean(xc * xc, axis=-1, keepdims=True)
    return xc * lax.rsqrt(var + eps) * w + b


def ssd_mixer(xbc, dt_raw, z, conv_w, conv_b, dt_bias, a_log, d_skip, norm_w):
    bsz, seq, _ = xbc.shape
    nc = seq // CHUNK
    G, R, P, N = SSD_GROUPS, SSD_HEADS_PER_GROUP, SSD_HEAD_DIM, SSD_STATE
    xbc = jax.nn.silu(causal_conv(xbc, conv_w.astype(F32), conv_b.astype(F32)))
    xs, bm, cm = jnp.split(xbc, [SSD_WIDTH, SSD_WIDTH + G * N], axis=-1)
    xs = xs.reshape(bsz, nc, CHUNK, G, R, P)
    bm = bm.reshape(bsz, nc, CHUNK, G, N)
    cm = cm.reshape(bsz, nc, CHUNK, G, N)
    dt = jax.nn.softplus(dt_raw + dt_bias.astype(F32)).reshape(bsz, nc, CHUNK, G, R)
    a_neg = -jnp.exp(a_log.astype(F32)).reshape(G, R)
    dt_t = jnp.moveaxis(dt, 2, -1)
    a_cum = jnp.cumsum(dt_t * a_neg[:, :, None], axis=-1)
    causal = jnp.tril(jnp.ones((CHUNK, CHUNK), bool))
    seg = a_cum[..., :, None] - a_cum[..., None, :]
    decay = jnp.exp(jnp.where(causal, seg, -jnp.inf))
    cb = jnp.einsum("bclgn,bcsgn->bcgls", cm, bm)
    wts = cb[:, :, :, None] * decay * dt_t[..., None, :]
    y_diag = jnp.einsum("bcgrls,bcsgrp->bclgrp", wts, xs)
    decay_states = jnp.exp(a_cum[..., -1:] - a_cum) * dt_t
    states = jnp.einsum("bclgn,bcgrl,bclgrp->bcgrpn", bm, decay_states, xs)
    chunk_decay = jnp.exp(a_cum[..., -1])

    def step(h, inp):
        s, dcy = inp
        return h * dcy[..., None, None] + s, h

    h0 = jnp.zeros((bsz, G, R, P, N), F32)
    _, prev = lax.scan(step, h0, (jnp.moveaxis(states, 1, 0), jnp.moveaxis(chunk_decay, 1, 0)))
    prev = jnp.moveaxis(prev, 0, 1)
    y_off = jnp.einsum("bclgn,bcgrpn->bclgrp", cm, prev) * jnp.moveaxis(jnp.exp(a_cum), -1, 2)[..., None]
    y = y_diag + y_off + d_skip.astype(F32).reshape(G, R)[:, :, None] * xs
    y = y.reshape(bsz, seq, SSD_WIDTH) * jax.nn.silu(z)
    y = rms_norm_last(y.reshape(bsz, seq, G, SSD_WIDTH // G)).reshape(bsz, seq, SSD_WIDTH)
    return y * norm_w


def _lin_combine(left, right):
    a1, b1 = left
    a2, b2 = right
    return a1 * a2, a2 * b1 + b2


def rglru_mixer(xl, gate, conv_w, conv_b, w_a, b_a, w_x, b_x, lam):
    bsz, seq, _ = xl.shape
    xc = causal_conv(xl, conv_w.astype(F32), conv_b.astype(F32))
    xb = xc.reshape(bsz, seq, LRU_BLOCKS, LRU_BLOCK_DIM)
    r = jax.nn.sigmoid(jnp.einsum("blhi,hij->blhj", xb, w_a.astype(F32)).reshape(bsz, seq, LRU_WIDTH) + b_a)
    i = jax.nn.sigmoid(jnp.einsum("blhi,hij->blhj", xb, w_x.astype(F32)).reshape(bsz, seq, LRU_WIDTH) + b_x)
    log_a = -LRU_C * r * jax.nn.softplus(-lam.astype(F32))
    a = jnp.exp(log_a)
    u = jnp.sqrt(-jnp.expm1(2.0 * log_a)) * (i * xc)
    _, h = lax.associative_scan(_lin_combine, (a, u), axis=1)
    return h * jax.nn.silu(gate)


def mlstm_mixer(xm, o_pre, gate, if_pre, conv_w, conv_b, w_q, w_k, w_v, b_i, b_f, norm_w):
    bsz, seq, _ = xm.shape
    nc = seq // CHUNK
    H, Dh = MLSTM_HEADS, MLSTM_HEAD_DIM
    xc = jax.nn.silu(causal_conv(xm, conv_w.astype(F32), conv_b.astype(F32))).reshape(bsz, seq, H, Dh)
    q = jnp.einsum("blhi,hij->bhlj", xc, w_q.astype(F32))
    k = jnp.einsum("blhi,hij->bhlj", xc, w_k.astype(F32)) * (Dh ** -0.5)
    v = jnp.einsum("blhi,hij->bhlj", xm.reshape(bsz, seq, H, Dh), w_v.astype(F32))
    i_pre, f_pre = jnp.split(if_pre, 2, axis=-1)
    i_pre = jnp.moveaxis(i_pre + b_i, -1, 1)
    log_f = jax.nn.log_sigmoid(jnp.moveaxis(f_pre + b_f, -1, 1))

    def to_chunks(t):
        t = t.reshape(t.shape[:2] + (nc, CHUNK) + t.shape[3:])
        return jnp.moveaxis(t, 2, 0)

    causal = jnp.tril(jnp.ones((CHUNK, CHUNK), bool))

    def body(carry, inp):
        c_st, n_st, m_st = carry
        qc, kc, vc, ic, lfc = inp
        bcum = jnp.cumsum(lfc, axis=-1)
        a_inter = bcum + m_st[..., None]
        d_intra = jnp.where(causal, bcum[..., :, None] - bcum[..., None, :] + ic[..., None, :], -jnp.inf)
        m_t = jnp.maximum(a_inter, jnp.max(d_intra, axis=-1))
        w_intra = jnp.exp(d_intra - m_t[..., None])
        w_inter = jnp.exp(a_inter - m_t)
        s = jnp.einsum("bhtd,bhsd->bhts", qc, kc) * w_intra
        num = jnp.einsum("bhts,bhse->bhte", s, vc) + w_inter[..., None] * jnp.einsum("bhtd,bhde->bhte", qc, c_st)
        den = jnp.sum(s, axis=-1) + w_inter * jnp.einsum("bhtd,bhd->bht", qc, n_st)
        h = num / jnp.maximum(jnp.abs(den), jnp.exp(-m_t))[..., None]
        m_new = m_t[..., -1]
        w_s = jnp.exp(bcum[..., -1:] - bcum + ic - m_new[..., None])
        carry_decay = jnp.exp(bcum[..., -1] + m_st - m_new)
        c_new = carry_decay[..., None, None] * c_st + jnp.einsum("bhs,bhsd,bhse->bhde", w_s, kc, vc)
        n_new = carry_decay[..., None] * n_st + jnp.einsum("bhs,bhsd->bhd", w_s, kc)
        return (c_new, n_new, m_new), h

    init = (jnp.zeros((bsz, H, Dh, Dh), F32), jnp.zeros((bsz, H, Dh), F32), jnp.zeros((bsz, H), F32))
    _, hs = lax.scan(body, init, (to_chunks(q), to_chunks(k), to_chunks(v), to_chunks(i_pre), to_chunks(log_f)))
    hs = jnp.transpose(hs, (1, 0, 3, 2, 4)).reshape(bsz, seq, H, Dh)
    hs = (rms_norm_last(hs) * norm_w.astype(F32).reshape(H, Dh)).reshape(bsz, seq, MLSTM_WIDTH)
    return hs * jax.nn.sigmoid(o_pre) * jax.nn.silu(gate)


def hybrid_layer(x, w_in, ssd_conv_w, ssd_conv_b, ssd_dt_bias, ssd_a_log, ssd_d, ssd_norm_w,
                 lru_conv_w, lru_conv_b, lru_w_a, lru_b_a, lru_w_x, lru_b_x, lru_lambda,
                 mlstm_conv_w, mlstm_conv_b, mlstm_w_q, mlstm_w_k, mlstm_w_v, mlstm_b_i, mlstm_b_f,
                 mlstm_norm_w, w_out, ln_w, ln_b):
    proj = jnp.einsum("bld,dn->bln", x, w_in).astype(F32)
    (ssd_z, ssd_xbc, ssd_dt, lru_x, lru_gate,
     ml_x, ml_o, ml_gate, ml_if) = jnp.split(proj, _split_points(), axis=-1)
    y_ssd = ssd_mixer(ssd_xbc, ssd_dt, ssd_z, ssd_conv_w, ssd_conv_b, ssd_dt_bias, ssd_a_log, ssd_d, ssd_norm_w)
    y_lru = rglru_mixer(lru_x, lru_gate, lru_conv_w, lru_conv_b, lru_w_a, lru_b_a, lru_w_x, lru_b_x, lru_lambda)
    y_ml = mlstm_mixer(ml_x, ml_o, ml_gate, ml_if, mlstm_conv_w, mlstm_conv_b, mlstm_w_q, mlstm_w_k,
                       mlstm_w_v, mlstm_b_i, mlstm_b_f, mlstm_norm_w)
    y = jnp.concatenate([y_ssd, y_lru, y_ml], axis=-1).astype(x.dtype)
    out = jnp.einsum("bln,nd->bld", y, w_out)
    return layer_norm(DEEPNORM_ALPHA * x.astype(F32) + out.astype(F32), ln_w, ln_b).astype(x.dtype)


def setup_inputs(seed: int = 0) -> dict:
    key = jax.random.key(seed)
    ks = jax.random.split(key, 26)

    def nrm(k, shape, s):
        return s * jax.random.normal(k, shape, F32)

    x = nrm(ks[0], (BATCH, SEQ, D_MODEL), 1.0)
    w_in = nrm(ks[1], (DEPTH, D_MODEL, N_IN), D_MODEL ** -0.5)
    ssd_conv_w = nrm(ks[2], (DEPTH, CONV_WIDTH, SSD_CONV_DIM), CONV_WIDTH ** -0.5)
    ssd_conv_b = nrm(ks[3], (DEPTH, SSD_CONV_DIM), 0.01)
    dt = jnp.exp(jax.random.uniform(ks[4], (DEPTH, SSD_HEADS), F32, math.log(1e-3), math.log(1e-1)))
    ssd_dt_bias = dt + jnp.log(-jnp.expm1(-dt))
    ssd_a_log = jnp.log(jax.random.uniform(ks[5], (DEPTH, SSD_HEADS), F32, 1.0, 16.0))
    ssd_d = 1.0 + nrm(ks[6], (DEPTH, SSD_HEADS), 0.01)
    ssd_norm_w = 1.0 + nrm(ks[7], (DEPTH, SSD_WIDTH), 0.01)
    lru_conv_w = nrm(ks[8], (DEPTH, CONV_WIDTH, LRU_WIDTH), CONV_WIDTH ** -0.5)
    lru_conv_b = nrm(ks[9], (DEPTH, LRU_WIDTH), 0.01)
    lru_w_a = nrm(ks[10], (DEPTH, LRU_BLOCKS, LRU_BLOCK_DIM, LRU_BLOCK_DIM), LRU_BLOCK_DIM ** -0.5)
    lru_b_a = nrm(ks[11], (DEPTH, LRU_WIDTH), 0.01)
    lru_w_x = nrm(ks[12], (DEPTH, LRU_BLOCKS, LRU_BLOCK_DIM, LRU_BLOCK_DIM), LRU_BLOCK_DIM ** -0.5)
    lru_b_x = nrm(ks[13], (DEPTH, LRU_WIDTH), 0.01)
    a_c = jax.random.uniform(ks[14], (DEPTH, LRU_WIDTH), F32, 0.9, 0.999)
    sig = a_c ** (1.0 / LRU_C)
    lru_lambda = jnp.log(sig) - jnp.log1p(-sig)
    mlstm_conv_w = nrm(ks[15], (DEPTH, CONV_WIDTH, MLSTM_WIDTH), CONV_WIDTH ** -0.5)
    mlstm_conv_b = nrm(ks[16], (DEPTH, MLSTM_WIDTH), 0.01)
    mlstm_w_q = nrm(ks[17], (DEPTH, MLSTM_HEADS, MLSTM_HEAD_DIM, MLSTM_HEAD_DIM), MLSTM_HEAD_DIM ** -0.5)
    mlstm_w_k = nrm(ks[18], (DEPTH, MLSTM_HEADS, MLSTM_HEAD_DIM, MLSTM_HEAD_DIM), MLSTM_HEAD_DIM ** -0.5)
    mlstm_w_v = nrm(ks[19], (DEPTH, MLSTM_HEADS, MLSTM_HEAD_DIM, MLSTM_HEAD_DIM), MLSTM_HEAD_DIM ** -0.5)
    mlstm_b_i = nrm(ks[20], (DEPTH, MLSTM_HEADS), 0.1)
    mlstm_b_f = jnp.linspace(3.0, 6.0, MLSTM_HEADS, dtype=F32)[None, :] + nrm(ks[21], (DEPTH, MLSTM_HEADS), 0.01)
    mlstm_norm_w = 1.0 + nrm(ks[22], (DEPTH, MLSTM_WIDTH), 0.01)
    w_out = nrm(ks[23], (DEPTH, MIX_WIDTH, D_MODEL), (MIX_WIDTH ** -0.5) * DEEPNORM_BETA)
    ln_w = 1.0 + nrm(ks[24], (DEPTH, D_MODEL), 0.01)
    ln_b = nrm(ks[25], (DEPTH, D_MODEL), 0.01)
    return {"x": x, "w_in": w_in, "ssd_conv_w": ssd_conv_w, "ssd_conv_b": ssd_conv_b,
            "ssd_dt_bias": ssd_dt_bias, "ssd_a_log": ssd_a_log, "ssd_d": ssd_d, "ssd_norm_w": ssd_norm_w,
            "lru_conv_w": lru_conv_w, "lru_conv_b": lru_conv_b, "lru_w_a": lru_w_a, "lru_b_a": lru_b_a,
            "lru_w_x": lru_w_x, "lru_b_x": lru_b_x, "lru_lambda": lru_lambda,
            "mlstm_conv_w": mlstm_conv_w, "mlstm_conv_b": mlstm_conv_b, "mlstm_w_q": mlstm_w_q,
            "mlstm_w_k": mlstm_w_k, "mlstm_w_v": mlstm_w_v, "mlstm_b_i": mlstm_b_i, "mlstm_b_f": mlstm_b_f,
            "mlstm_norm_w": mlstm_norm_w, "w_out": w_out, "ln_w": ln_w, "ln_b": ln_b}


def reference(x, w_in, ssd_conv_w, ssd_conv_b, ssd_dt_bias, ssd_a_log, ssd_d, ssd_norm_w,
              lru_conv_w, lru_conv_b, lru_w_a, lru_b_a, lru_w_x, lru_b_x, lru_lambda,
              mlstm_conv_w, mlstm_conv_b, mlstm_w_q, mlstm_w_k, mlstm_w_v, mlstm_b_i, mlstm_b_f,
              mlstm_norm_w, w_out, ln_w, ln_b):
    for l in range(DEPTH):
        x = hybrid_layer(x, w_in[l], ssd_conv_w[l], ssd_conv_b[l], ssd_dt_bias[l], ssd_a_log[l], ssd_d[l],
                         ssd_norm_w[l], lru_conv_w[l], lru_conv_b[l], lru_w_a[l], lru_b_a[l], lru_w_x[l],
                         lru_b_x[l], lru_lambda[l], mlstm_conv_w[l], mlstm_conv_b[l], mlstm_w_q[l],
                         mlstm_w_k[l], mlstm_w_v[l], mlstm_b_i[l], mlstm_b_f[l], mlstm_norm_w[l],
                         w_out[l], ln_w[l], ln_b[l])
    return x
```

```python
import functools
import math

import jax
import jax.numpy as jnp
from jax import lax
from jax.experimental import pallas as pl
from jax.experimental.pallas import tpu as pltpu

F32 = jnp.float32
BF16 = jnp.bfloat16

D_MODEL = 4096
DEPTH = 2
MIX_WIDTH = 2 * D_MODEL
SSD_WIDTH = MIX_WIDTH // 2
LRU_WIDTH = MIX_WIDTH // 4
MLSTM_WIDTH = MIX_WIDTH - SSD_WIDTH - LRU_WIDTH
SSD_HEAD_DIM = 64
SSD_HEAD_SHIFT = 6
SSD_HEADS = SSD_WIDTH // SSD_HEAD_DIM
SSD_GROUPS = 8
SSD_HPG = SSD_HEADS // SSD_GROUPS
SSD_STATE = 128
SSD_GROUP_WIDTH = SSD_WIDTH // SSD_GROUPS
LRU_BLOCKS = 16
LRU_BLOCK_DIM = LRU_WIDTH // LRU_BLOCKS
LRU_C = 8.0
MLSTM_HEADS = 8
MLSTM_HEAD_DIM = MLSTM_WIDTH // MLSTM_HEADS
CONV_WIDTH = 4
CHUNK = 128
DEEPNORM_ALPHA = (2.0 * DEPTH) ** 0.25

SUBLANES = 8
LANES = 128

N_MAIN = SSD_WIDTH + (SSD_WIDTH + 2 * SSD_GROUPS * SSD_STATE) + 2 * LRU_WIDTH + 3 * MLSTM_WIDTH
COL_Z = 0
COL_XS = SSD_WIDTH
COL_B = COL_XS + SSD_WIDTH
COL_C = COL_B + SSD_GROUPS * SSD_STATE
COL_LRU_X = COL_C + SSD_GROUPS * SSD_STATE
COL_LRU_G = COL_LRU_X + LRU_WIDTH
COL_ML_X = COL_LRU_G + LRU_WIDTH
COL_ML_O = COL_ML_X + MLSTM_WIDTH
COL_ML_G = COL_ML_O + MLSTM_WIDTH
SMALL_I = SSD_HEADS
SMALL_F = SSD_HEADS + MLSTM_HEADS

VMEM_LIMIT = 56 * 1024 * 1024
TIME_BLOCK = 512
CHUNKS_PER_BLOCK = TIME_BLOCK // CHUNK


def _sigmoid(x):
    return 1.0 / (1.0 + jnp.exp(-x))


def _silu(x):
    return x * _sigmoid(x)


def _softplus(x):
    return jnp.maximum(x, 0.0) + jnp.log1p(jnp.exp(-jnp.abs(x)))


def _dot(a, b):
    return jnp.dot(a, b, preferred_element_type=F32)


def _dot_nt(a, b):
    return lax.dot_general(a, b, (((1,), (1,)), ((), ())), preferred_element_type=F32)


def _dot_f32(a, b):
    return jnp.dot(a, b, precision=lax.Precision.HIGHEST, preferred_element_type=F32)


def _iota(shape, dim):
    return lax.broadcasted_iota(jnp.int32, shape, dim)


def _lower_tri():
    return (_iota((CHUNK, CHUNK), 1) <= _iota((CHUNK, CHUNK), 0)).astype(F32)


def _column(x, idx):
    lane = _iota(x.shape, 1)
    return jnp.sum(jnp.where(lane == idx, x, 0.0), axis=1, keepdims=True)


def _conv_block(x_ref, xp_scr, w_ref, b_ref, out_scr, first, width, apply_silu):
    tb = TIME_BLOCK

    @pl.when(first)
    def _():
        xp_scr[0:SUBLANES, :] = jnp.zeros((SUBLANES, width), F32)

    @pl.when(jnp.logical_not(first))
    def _():
        xp_scr[0:SUBLANES, :] = xp_scr[tb:tb + SUBLANES, :]

    xp_scr[SUBLANES:SUBLANES + tb, :] = x_ref[...]
    for c in range(CHUNKS_PER_BLOCK):
        r0 = c * CHUNK
        for l0 in range(0, width, LANES):
            acc = b_ref[:, l0:l0 + LANES] + w_ref[3:4, l0:l0 + LANES] * xp_scr[r0 + 8:r0 + 8 + CHUNK, l0:l0 + LANES]
            for k in range(CONV_WIDTH - 1):
                off = r0 + 8 - (CONV_WIDTH - 1) + k
                acc = acc + w_ref[k:k + 1, l0:l0 + LANES] * xp_scr[off:off + CHUNK, l0:l0 + LANES]
            if apply_silu:
                acc = _silu(acc)
            out_scr[r0:r0 + CHUNK, l0:l0 + LANES] = acc


def _mm_kernel(a_ref, b_ref, o_ref):
    o_ref[...] = _dot(a_ref[...], b_ref[...])


def _matmul(a, b, tm, tn):
    m, k = a.shape
    _, n = b.shape
    return pl.pallas_call(
        _mm_kernel,
        out_shape=jax.ShapeDtypeStruct((m, n), F32),
        grid=(m // tm, n // tn),
        in_specs=[pl.BlockSpec((tm, k), lambda i, j: (i, 0)),
                  pl.BlockSpec((k, tn), lambda i, j: (0, j))],
        out_specs=pl.BlockSpec((tm, tn), lambda i, j: (i, j)),
        compiler_params=pltpu.CompilerParams(
            dimension_semantics=("arbitrary", "arbitrary"), vmem_limit_bytes=VMEM_LIMIT),
        name="in_proj",
    )(a, b)


def _ssd_kernel(z_ref, xs_ref, bm_ref, cm_ref, sm_ref,
                cwx_ref, cwb_ref, cwc_ref, cbx_ref, cbb_ref, cbc_ref,
                smb_ref, alog_ref, dexp_ref, nw_ref,
                y_ref,
                xpx_scr, xpb_scr, xpc_scr, xcx_scr, xcb_scr, xcc_scr, st_scr, tr_scr):
    g = pl.program_id(1)
    first = pl.program_id(2) == 0
    gw = SSD_GROUP_WIDTH

    @pl.when(first)
    def _():
        st_scr[...] = jnp.zeros(st_scr.shape, F32)

    _conv_block(xs_ref, xpx_scr, cwx_ref, cbx_ref, xcx_scr, first, gw, True)
    _conv_block(bm_ref, xpb_scr, cwb_ref, cbb_ref, xcb_scr, first, SSD_STATE, True)
    _conv_block(cm_ref, xpc_scr, cwc_ref, cbc_ref, xcc_scr, first, SSD_STATE, True)

    ltri = _lower_tri()
    row = _iota((CHUNK, CHUNK), 0)
    col = _iota((CHUNK, CHUNK), 1)
    causal = col <= row
    lane = _iota((1, LANES), 1)
    a_neg = jnp.where(lane < SSD_HEADS, -jnp.exp(alog_ref[...]), 0.0)
    expand = ((jnp.right_shift(_iota((LANES, gw), 1), SSD_HEAD_SHIFT) + g * SSD_HPG)
              == _iota((LANES, gw), 0)).astype(F32).astype(BF16)
    low_half = col < SSD_HEAD_DIM

    def chunk(c, carry):
        r0 = pl.multiple_of(c * CHUNK, CHUNK)
        xs = xcx_scr[pl.ds(r0, CHUNK), :]
        bm = xcb_scr[pl.ds(r0, CHUNK), :]
        cm = xcc_scr[pl.ds(r0, CHUNK), :]
        bm16 = bm.astype(BF16)
        cm16 = cm.astype(BF16)

        dt_all = _softplus(sm_ref[pl.ds(r0, CHUNK), :] + smb_ref[...])
        da_all = dt_all * a_neg
        acum_all = _dot_f32(ltri, da_all)
        tr_scr[0] = dt_all.T
        tr_scr[1] = acum_all.T
        hrow = pl.multiple_of(g * SSD_HPG, SSD_HPG)
        dt_t = tr_scr[0, pl.ds(hrow, SSD_HPG), :]
        acum_t = tr_scr[1, pl.ds(hrow, SSD_HPG), :]

        cb = _dot_nt(cm16, bm16)

        ydiag = []
        for j in range(SSD_HPG // 2):
            wts = []
            for r in (2 * j, 2 * j + 1):
                acol = _column(acum_all, g * SSD_HPG + r)
                seg = acol - acum_t[r:r + 1, :]
                decay = jnp.exp(jnp.where(causal, seg, -jnp.inf))
                wts.append((cb * decay * dt_t[r:r + 1, :]).astype(BF16))
            lhs = jnp.concatenate(wts, axis=1)
            xp = xs[:, j * LANES:(j + 1) * LANES]
            rhs = jnp.concatenate([jnp.where(low_half, xp, 0.0).astype(BF16),
                                   jnp.where(low_half, 0.0, xp).astype(BF16)], axis=0)
            ydiag.append(_dot(lhs, rhs))
        y = jnp.concatenate(ydiag, axis=1)

        a_last = acum_all[CHUNK - 1:CHUNK, :]
        q1 = jnp.exp(a_last - acum_all) * dt_all
        q2 = jnp.exp(acum_all)
        q1h = q1.astype(BF16)
        q2h = q2.astype(BF16)
        q1l = (q1 - q1h.astype(F32)).astype(BF16)
        q2l = (q2 - q2h.astype(F32)).astype(BF16)
        qe = _dot(jnp.concatenate([q1h, q1l, q2h, q2l], axis=0), expand)
        q1e = qe[0:CHUNK] + qe[CHUNK:2 * CHUNK]
        q2e = qe[2 * CHUNK:3 * CHUNK] + qe[3 * CHUNK:4 * CHUNK]

        prev = st_scr[...]
        states = _dot(bm.T.astype(BF16), (q1e * xs).astype(BF16))
        y = y + _dot(cm16, prev.astype(BF16)) * q2e
        st_scr[...] = prev * q2e[CHUNK - 1:CHUNK, :] + states

        y = y + dexp_ref[...] * xs
        y = y * _silu(z_ref[pl.ds(r0, CHUNK), :])
        y = y * lax.rsqrt(jnp.mean(y * y, axis=1, keepdims=True) + 1e-6)
        y_ref[pl.ds(r0, CHUNK), :] = (y * nw_ref[...]).astype(BF16)
        return carry

    lax.fori_loop(0, CHUNKS_PER_BLOCK, chunk, 0)


def _ssd_mixer(proj, small, batch, seq, conv_w, conv_b, small_bias, alog_row, d_exp, norm_w):
    t = proj.shape[0]
    nb = seq // TIME_BLOCK
    tb = TIME_BLOCK
    gw = SSD_GROUP_WIDTH

    def rows(b, g, i):
        return b * nb + i

    def col_spec(width, col0):
        base = col0 // width
        return pl.BlockSpec((tb, width), lambda b, g, i: (rows(b, g, i), base + g))

    def w_spec(nrows, width, col0):
        base = col0 // width
        return pl.BlockSpec((nrows, width), lambda b, g, i: (0, base + g))

    const = pl.BlockSpec((1, LANES), lambda b, g, i: (0, 0))
    cx, cb_, cc = 0, SSD_WIDTH, SSD_WIDTH + SSD_GROUPS * SSD_STATE
    return pl.pallas_call(
        _ssd_kernel,
        out_shape=jax.ShapeDtypeStruct((t, SSD_WIDTH), BF16),
        grid=(batch, SSD_GROUPS, nb),
        in_specs=[col_spec(gw, COL_Z), col_spec(gw, COL_XS), col_spec(SSD_STATE, COL_B),
                  col_spec(SSD_STATE, COL_C),
                  pl.BlockSpec((tb, LANES), lambda b, g, i: (rows(b, g, i), 0)),
                  w_spec(CONV_WIDTH, gw, cx), w_spec(CONV_WIDTH, SSD_STATE, cb_),
                  w_spec(CONV_WIDTH, SSD_STATE, cc),
                  w_spec(1, gw, cx), w_spec(1, SSD_STATE, cb_), w_spec(1, SSD_STATE, cc),
                  const, const, w_spec(1, gw, 0), w_spec(1, gw, 0)],
        out_specs=pl.BlockSpec((tb, gw), lambda b, g, i: (rows(b, g, i), g)),
        scratch_shapes=[pltpu.VMEM((tb + 2 * SUBLANES, gw), F32),
                        pltpu.VMEM((tb + 2 * SUBLANES, SSD_STATE), F32),
                        pltpu.VMEM((tb + 2 * SUBLANES, SSD_STATE), F32),
                        pltpu.VMEM((tb, gw), F32),
                        pltpu.VMEM((tb, SSD_STATE), F32),
                        pltpu.VMEM((tb, SSD_STATE), F32),
                        pltpu.VMEM((SSD_STATE, gw), F32),
                        pltpu.VMEM((2, LANES, LANES), F32)],
        compiler_params=pltpu.CompilerParams(
            dimension_semantics=("arbitrary", "arbitrary", "arbitrary"), vmem_limit_bytes=VMEM_LIMIT),
        name="ssd_mixer",
    )(proj, proj, proj, proj, small, conv_w, conv_w, conv_w, conv_b, conv_b, conv_b,
      small_bias, alog_row, d_exp, norm_w)


def _lru_kernel(x_ref, gate_ref, cw_ref, cb_ref, wa_ref, wx_ref, ba_ref, bx_ref, lam_ref,
                y_ref, xp_scr, xc_scr, h_scr):
    first = pl.program_id(2) == 0

    @pl.when(first)
    def _():
        h_scr[...] = jnp.zeros(h_scr.shape, F32)

    _conv_block(x_ref, xp_scr, cw_ref, cb_ref, xc_scr, first, LRU_BLOCK_DIM, False)
    row = _iota((CHUNK, LANES), 0)
    sp_lam = _softplus(-lam_ref[...])

    def chunk(c, carry):
        r0 = pl.multiple_of(c * CHUNK, CHUNK)
        xc = xc_scr[pl.ds(r0, CHUNK), :]
        xc16 = xc.astype(BF16)
        r = _sigmoid(_dot(xc16, wa_ref[0]) + ba_ref[...])
        i = _sigmoid(_dot(xc16, wx_ref[0]) + bx_ref[...])
        log_a = -LRU_C * r * sp_lam
        a = jnp.exp(log_a)
        u = jnp.sqrt(-jnp.tanh(log_a) * (a * a + 1.0)) * (i * xc)
        d = 1
        while d < CHUNK:
            keep = row >= d
            a_sh = pltpu.roll(a, d, 0)
            u_sh = pltpu.roll(u, d, 0)
            u = jnp.where(keep, u + a * u_sh, u)
            a = jnp.where(keep, a * a_sh, a)
            d *= 2
        h = u + a * h_scr[...]
        h_scr[...] = h[CHUNK - 1:CHUNK, :]
        y_ref[pl.ds(r0, CHUNK), :] = (h * _silu(gate_ref[pl.ds(r0, CHUNK), :])).astype(BF16)
        return carry

    lax.fori_loop(0, CHUNKS_PER_BLOCK, chunk, 0)


def _lru_mixer(proj, batch, seq, conv_w, conv_b, w_a, w_x, b_a, b_x, lam):
    t = proj.shape[0]
    nb = seq // TIME_BLOCK
    tb = TIME_BLOCK
    bd = LRU_BLOCK_DIM

    def col_spec(col0):
        base = col0 // bd
        return pl.BlockSpec((tb, bd), lambda b, h, i: (b * nb + i, base + h))

    vec = pl.BlockSpec((1, bd), lambda b, h, i: (0, h))
    wspec = pl.BlockSpec((1, bd, bd), lambda b, h, i: (h, 0, 0))
    return pl.pallas_call(
        _lru_kernel,
        out_shape=jax.ShapeDtypeStruct((t, LRU_WIDTH), BF16),
        grid=(batch, LRU_BLOCKS, nb),
        in_specs=[col_spec(COL_LRU_X), col_spec(COL_LRU_G),
                  pl.BlockSpec((CONV_WIDTH, bd), lambda b, h, i: (0, h)), vec,
                  wspec, wspec, vec, vec, vec],
        out_specs=pl.BlockSpec((tb, bd), lambda b, h, i: (b * nb + i, h)),
        scratch_shapes=[pltpu.VMEM((tb + 2 * SUBLANES, bd), F32),
                        pltpu.VMEM((tb, bd), F32),
                        pltpu.VMEM((1, bd), F32)],
        compiler_params=pltpu.CompilerParams(
            dimension_semantics=("arbitrary", "arbitrary", "arbitrary"), vmem_limit_bytes=VMEM_LIMIT),
        name="lru_mixer",
    )(proj, proj, conv_w, conv_b, w_a, w_x, b_a, b_x, lam)


def _mlstm_kernel(x_ref, o_ref, gate_ref, sm_ref, cw_ref, cb_ref, smb_ref,
                  wq_ref, wk_ref, wv_ref, nw_ref,
                  y_ref, xp_scr, xc_scr, c_scr, n_scr, m_scr, tr_scr):
    h = pl.program_id(1)
    first = pl.program_id(2) == 0
    dh = MLSTM_HEAD_DIM

    @pl.when(first)
    def _():
        c_scr[...] = jnp.zeros(c_scr.shape, F32)
        n_scr[...] = jnp.zeros(n_scr.shape, F32)
        m_scr[...] = jnp.zeros(m_scr.shape, F32)

    _conv_block(x_ref, xp_scr, cw_ref, cb_ref, xc_scr, first, dh, True)
    ltri = _lower_tri()
    causal = _iota((CHUNK, CHUNK), 1) <= _iota((CHUNK, CHUNK), 0)

    def chunk(c, carry):
        r0 = pl.multiple_of(c * CHUNK, CHUNK)
        xm16 = x_ref[pl.ds(r0, CHUNK), :].astype(BF16)
        xc16 = xc_scr[pl.ds(r0, CHUNK), :].astype(BF16)
        q = _dot(xc16, wq_ref[0])
        k = _dot(xc16, wk_ref[0]) * (dh ** -0.5)
        v = _dot(xm16, wv_ref[0])
        q16 = q.astype(BF16)
        v16 = v.astype(BF16)

        pre = sm_ref[pl.ds(r0, CHUNK), :] + smb_ref[...]
        bcum_all = _dot_f32(ltri, -_softplus(-pre))
        tr_scr[0] = pre.T
        tr_scr[1] = bcum_all.T
        i_row = tr_scr[0, pl.ds(SMALL_I + h, 1), :]
        b_row = tr_scr[1, pl.ds(SMALL_F + h, 1), :]
        i_col = _column(pre, SMALL_I + h)
        b_col = _column(bcum_all, SMALL_F + h)

        m_prev = m_scr[...]
        a_inter = b_col + m_prev
        d_intra = jnp.where(causal, b_col - b_row + i_row, -jnp.inf)
        m_t = jnp.maximum(a_inter, jnp.max(d_intra, axis=1, keepdims=True))
        w_intra = jnp.exp(d_intra - m_t)
        w_inter = jnp.exp(a_inter - m_t)
        s = _dot_nt(q16, k.astype(BF16)) * w_intra
        c_st = c_scr[...]
        n_st = n_scr[...]
        num = _dot(s.astype(BF16), v16) + w_inter * _dot(q16, c_st.astype(BF16))
        den = jnp.sum(s, axis=1, keepdims=True) + w_inter * jnp.sum(q * n_st, axis=1, keepdims=True)
        hs = num / jnp.maximum(jnp.abs(den), jnp.exp(-m_t))

        m_new = m_t[CHUNK - 1:CHUNK, :]
        b_last = b_col[CHUNK - 1:CHUNK, :]
        w_s = jnp.exp(b_last - b_col + i_col - m_new)
        carry_decay = jnp.exp(b_last + m_prev - m_new)
        kw = k * w_s
        c_scr[...] = carry_decay * c_st + _dot(kw.T.astype(BF16), v16)
        n_scr[...] = carry_decay * n_st + jnp.sum(kw, axis=0, keepdims=True)
        m_scr[...] = m_new

        hn = hs * lax.rsqrt(jnp.mean(hs * hs, axis=1, keepdims=True) + 1e-6) * nw_ref[...]
        out = hn * _sigmoid(o_ref[pl.ds(r0, CHUNK), :]) * _silu(gate_ref[pl.ds(r0, CHUNK), :])
        y_ref[pl.ds(r0, CHUNK), :] = out.astype(BF16)
        return carry

    lax.fori_loop(0, CHUNKS_PER_BLOCK, chunk, 0)


def _mlstm_mixer(proj, small, batch, seq, conv_w, conv_b, small_bias, w_q, w_k, w_v, norm_w):
    t = proj.shape[0]
    nb = seq // TIME_BLOCK
    tb = TIME_BLOCK
    dh = MLSTM_HEAD_DIM

    def col_spec(col0):
        base = col0 // dh
        return pl.BlockSpec((tb, dh), lambda b, h, i: (b * nb + i, base + h))

    vec = pl.BlockSpec((1, dh), lambda b, h, i: (0, h))
    wspec = pl.BlockSpec((1, dh, dh), lambda b, h, i: (h, 0, 0))
    return pl.pallas_call(
        _mlstm_kernel,
        out_shape=jax.ShapeDtypeStruct((t, MLSTM_WIDTH), BF16),
        grid=(batch, MLSTM_HEADS, nb),
        in_specs=[col_spec(COL_ML_X), col_spec(COL_ML_O), col_spec(COL_ML_G),
                  pl.BlockSpec((tb, LANES), lambda b, h, i: (b * nb + i, 0)),
                  pl.BlockSpec((CONV_WIDTH, dh), lambda b, h, i: (0, h)), vec,
                  pl.BlockSpec((1, LANES), lambda b, h, i: (0, 0)),
                  wspec, wspec, wspec, vec],
        out_specs=pl.BlockSpec((tb, dh), lambda b, h, i: (b * nb + i, h)),
        scratch_shapes=[pltpu.VMEM((tb + 2 * SUBLANES, dh), F32),
                        pltpu.VMEM((tb, dh), F32),
                        pltpu.VMEM((dh, dh), F32),
                        pltpu.VMEM((1, dh), F32),
                        pltpu.VMEM((1, 1), F32),
                        pltpu.VMEM((2, LANES, LANES), F32)],
        compiler_params=pltpu.CompilerParams(
            dimension_semantics=("arbitrary", "arbitrary", "arbitrary"), vmem_limit_bytes=VMEM_LIMIT),
        name="mlstm_mixer",
    )(proj, proj, proj, small, conv_w, conv_b, small_bias, w_q, w_k, w_v, norm_w)


OUT_TM = 512
OUT_TK = 512
OUT_K_SSD = SSD_WIDTH // OUT_TK
OUT_K_LRU = LRU_WIDTH // OUT_TK
OUT_K_ML = MLSTM_WIDTH // OUT_TK
OUT_K_STEPS = OUT_K_SSD + OUT_K_LRU + OUT_K_ML
LN_ROWS = 64


def _out_ln_kernel(ys_ref, yl_ref, ym_ref, w_ref, x_ref, lnw_ref, lnb_ref, of_ref, ob_ref):
    k = pl.program_id(1)

    @pl.when(k == 0)
    def _():
        of_ref[...] = DEEPNORM_ALPHA * x_ref[...]

    @pl.when(k < OUT_K_SSD)
    def _():
        of_ref[...] += _dot(ys_ref[...], w_ref[...])

    @pl.when(jnp.logical_and(k >= OUT_K_SSD, k < OUT_K_SSD + OUT_K_LRU))
    def _():
        of_ref[...] += _dot(yl_ref[...], w_ref[...])

    @pl.when(k >= OUT_K_SSD + OUT_K_LRU)
    def _():
        of_ref[...] += _dot(ym_ref[...], w_ref[...])

    @pl.when(k == OUT_K_STEPS - 1)
    def _():
        def ln(r, carry):
            r0 = pl.multiple_of(r * LN_ROWS, LN_ROWS)
            v = of_ref[pl.ds(r0, LN_ROWS), :]
            mu = jnp.mean(v, axis=1, keepdims=True)
            vc = v - mu
            var = jnp.mean(vc * vc, axis=1, keepdims=True)
            o = vc * lax.rsqrt(var + 1e-5) * lnw_ref[...] + lnb_ref[...]
            of_ref[pl.ds(r0, LN_ROWS), :] = o
            ob_ref[pl.ds(r0, LN_ROWS), :] = o.astype(BF16)
            return carry

        lax.fori_loop(0, OUT_TM // LN_ROWS, ln, 0)


def _out_ln(y_ssd, y_lru, y_ml, w_out, x, ln_w, ln_b):
    t = x.shape[0]
    tm, tk = OUT_TM, OUT_TK
    k1, k2 = OUT_K_SSD, OUT_K_SSD + OUT_K_LRU
    vec = pl.BlockSpec((1, D_MODEL), lambda i, k: (0, 0))
    return pl.pallas_call(
        _out_ln_kernel,
        out_shape=(jax.ShapeDtypeStruct((t, D_MODEL), F32), jax.ShapeDtypeStruct((t, D_MODEL), BF16)),
        grid=(t // tm, OUT_K_STEPS),
        in_specs=[pl.BlockSpec((tm, tk), lambda i, k: (i, jnp.minimum(k, k1 - 1))),
                  pl.BlockSpec((tm, tk), lambda i, k: (i, jnp.clip(k - k1, 0, OUT_K_LRU - 1))),
                  pl.BlockSpec((tm, tk), lambda i, k: (i, jnp.clip(k - k2, 0, OUT_K_ML - 1))),
                  pl.BlockSpec((tk, D_MODEL), lambda i, k: (k, 0)),
                  pl.BlockSpec((tm, D_MODEL), lambda i, k: (i, 0)),
                  vec, vec],
        out_specs=(pl.BlockSpec((tm, D_MODEL), lambda i, k: (i, 0)),
                   pl.BlockSpec((tm, D_MODEL), lambda i, k: (i, 0))),
        compiler_params=pltpu.CompilerParams(
            dimension_semantics=("arbitrary", "arbitrary"), vmem_limit_bytes=VMEM_LIMIT),
        name="out_proj_ln",
    )(y_ssd, y_lru, y_ml, w_out, x, ln_w, ln_b)


def _split_in_weight(w_in):
    s0 = SSD_WIDTH + (SSD_WIDTH + 2 * SSD_GROUPS * SSD_STATE)
    s1 = s0 + SSD_HEADS
    s2 = s1 + 2 * LRU_WIDTH + 3 * MLSTM_WIDTH
    w_main = jnp.concatenate([w_in[:, :s0], w_in[:, s1:s2]], axis=1).astype(BF16)
    pad = jnp.zeros((w_in.shape[0], LANES - SSD_HEADS - 2 * MLSTM_HEADS), w_in.dtype)
    w_small = jnp.concatenate([w_in[:, s0:s1], w_in[:, s2:], pad], axis=1).astype(BF16)
    return w_main, w_small


def _layer(x, x16, batch, seq, w_in, ssd_conv_w, ssd_conv_b, ssd_dt_bias, ssd_a_log, ssd_d, ssd_norm_w,
           lru_conv_w, lru_conv_b, lru_w_a, lru_b_a, lru_w_x, lru_b_x, lru_lambda,
           mlstm_conv_w, mlstm_conv_b, mlstm_w_q, mlstm_w_k, mlstm_w_v, mlstm_b_i, mlstm_b_f,
           mlstm_norm_w, w_out, ln_w, ln_b):
    w_main, w_small = _split_in_weight(w_in)
    proj = _matmul(x16, w_main, 1024, 512)
    small = _matmul(x16, w_small, 1024, LANES)

    zpad = jnp.zeros((LANES - SSD_HEADS - 2 * MLSTM_HEADS,), F32)
    small_bias = jnp.concatenate([ssd_dt_bias, mlstm_b_i, mlstm_b_f, zpad]).reshape(1, LANES)
    alog_row = jnp.concatenate([ssd_a_log, jnp.zeros((LANES - SSD_HEADS,), F32)]).reshape(1, LANES)
    d_exp = jnp.repeat(ssd_d, SSD_HEAD_DIM).reshape(1, SSD_WIDTH)

    y_ssd = _ssd_mixer(proj, small, batch, seq, ssd_conv_w, ssd_conv_b.reshape(1, -1), small_bias,
                       alog_row, d_exp, ssd_norm_w.reshape(1, -1))
    y_lru = _lru_mixer(proj, batch, seq, lru_conv_w, lru_conv_b.reshape(1, -1),
                       lru_w_a.astype(BF16), lru_w_x.astype(BF16),
                       lru_b_a.reshape(1, -1), lru_b_x.reshape(1, -1), lru_lambda.reshape(1, -1))
    y_ml = _mlstm_mixer(proj, small, batch, seq, mlstm_conv_w, mlstm_conv_b.reshape(1, -1), small_bias,
                        mlstm_w_q.astype(BF16), mlstm_w_k.astype(BF16), mlstm_w_v.astype(BF16),
                        mlstm_norm_w.reshape(1, -1))
    return _out_ln(y_ssd, y_lru, y_ml, w_out.astype(BF16), x, ln_w.reshape(1, -1), ln_b.reshape(1, -1))


def kernel(x, w_in, ssd_conv_w, ssd_conv_b, ssd_dt_bias, ssd_a_log, ssd_d, ssd_norm_w, lru_conv_w, lru_conv_b,
           lru_w_a, lru_b_a, lru_w_x, lru_b_x, lru_lambda, mlstm_conv_w, mlstm_conv_b, mlstm_w_q, mlstm_w_k,
           mlstm_w_v, mlstm_b_i, mlstm_b_f, mlstm_norm_w, w_out, ln_w, ln_b):
    batch, seq, d = x.shape
    assert d == D_MODEL and seq % TIME_BLOCK == 0 and w_in.shape[0] == DEPTH
    xf = x.reshape(batch * seq, d)
    x16 = xf.astype(BF16)
    params = (w_in, ssd_conv_w, ssd_conv_b, ssd_dt_bias, ssd_a_log, ssd_d, ssd_norm_w, lru_conv_w, lru_conv_b,
              lru_w_a, lru_b_a, lru_w_x, lru_b_x, lru_lambda, mlstm_conv_w, mlstm_conv_b, mlstm_w_q,
              mlstm_w_k, mlstm_w_v, mlstm_b_i, mlstm_b_f, mlstm_norm_w, w_out, ln_w, ln_b)
    for l in range(DEPTH):
        xf, x16 = _layer(xf, x16, batch, seq, *(p[l] for p in params))
    return xf.reshape(batch, seq, d)
```

```python
import jax
import jax.numpy as jnp
from jax import lax
from jax.experimental import pallas as pl
from jax.experimental.pallas import tpu as pltpu

F32 = jnp.float32
BF16 = jnp.bfloat16

D_MODEL = 4096
DEPTH = 2
MIX_WIDTH = 2 * D_MODEL
SSD_WIDTH = MIX_WIDTH // 2
LRU_WIDTH = MIX_WIDTH // 4
MLSTM_WIDTH = MIX_WIDTH - SSD_WIDTH - LRU_WIDTH
SSD_HEAD_DIM = 64
SSD_HEAD_SHIFT = 6
SSD_HEADS = SSD_WIDTH // SSD_HEAD_DIM
SSD_GROUPS = 8
SSD_HPG = SSD_HEADS // SSD_GROUPS
SSD_STATE = 128
SSD_GROUP_WIDTH = SSD_WIDTH // SSD_GROUPS
LRU_BLOCKS = 16
LRU_BLOCK_DIM = LRU_WIDTH // LRU_BLOCKS
LRU_C = 8.0
MLSTM_HEADS = 8
MLSTM_HEAD_DIM = MLSTM_WIDTH // MLSTM_HEADS
CONV_WIDTH = 4
CHUNK = 128
DEEPNORM_ALPHA = (2.0 * DEPTH) ** 0.25

SUBLANES = 8
LANES = 128
HALF_LANES = LANES // 2

XBC_WIDTH = SSD_WIDTH + 2 * SSD_GROUPS * SSD_STATE
N_HEAD = SSD_WIDTH + XBC_WIDTH
N_TAIL = 2 * LRU_WIDTH + 3 * MLSTM_WIDTH
N_MAIN = N_HEAD + N_TAIL
N_IN = N_HEAD + SSD_HEADS + N_TAIL + 2 * MLSTM_HEADS
COL_Z = 0
COL_XS = SSD_WIDTH
COL_B = COL_XS + SSD_WIDTH
COL_C = COL_B + SSD_GROUPS * SSD_STATE
COL_LRU_X = COL_C + SSD_GROUPS * SSD_STATE
COL_LRU_G = COL_LRU_X + LRU_WIDTH
COL_ML_X = COL_LRU_G + LRU_WIDTH
COL_ML_O = COL_ML_X + MLSTM_WIDTH
COL_ML_G = COL_ML_O + MLSTM_WIDTH
SMALL_I = SSD_HEADS
SMALL_F = SSD_HEADS + MLSTM_HEADS

VMEM_LIMIT = 56 * 1024 * 1024
TIME_BLOCK = 512
CHUNKS_PER_BLOCK = TIME_BLOCK // CHUNK


def _params(n_axes):
    return pltpu.CompilerParams(dimension_semantics=("arbitrary",) * n_axes, vmem_limit_bytes=VMEM_LIMIT)


def _sigmoid(x):
    return 0.5 * jnp.tanh(0.5 * x) + 0.5


def _silu(x):
    return x * _sigmoid(x)


def _softplus(x):
    return jnp.maximum(x, 0.0) + jnp.log1p(jnp.exp(-jnp.abs(x)))


def _dot(a, b):
    return jnp.dot(a, b, preferred_element_type=F32)


def _dot_nt(a, b):
    return lax.dot_general(a, b, (((1,), (1,)), ((), ())), preferred_element_type=F32)


def _dot_f32(a, b):
    return jnp.dot(a, b, precision=lax.Precision.HIGHEST, preferred_element_type=F32)


def _iota(shape, dim):
    return lax.broadcasted_iota(jnp.int32, shape, dim)


def _column(x, idx):
    lane = _iota(x.shape, 1)
    return jnp.sum(jnp.where(lane == idx, x, 0.0), axis=1, keepdims=True)


def _conv_block(x_ref, xp_scr, w_ref, b_ref, out_scr, first, width, apply_silu):
    tb = TIME_BLOCK

    @pl.when(first)
    def _():
        xp_scr[0:SUBLANES, :] = jnp.zeros((SUBLANES, width), F32)

    @pl.when(jnp.logical_not(first))
    def _():
        xp_scr[0:SUBLANES, :] = xp_scr[tb:tb + SUBLANES, :]

    xp_scr[SUBLANES:SUBLANES + tb, :] = x_ref[...]
    for c in range(CHUNKS_PER_BLOCK):
        r0 = c * CHUNK
        for l0 in range(0, width, LANES):
            acc = b_ref[:, l0:l0 + LANES] + w_ref[3:4, l0:l0 + LANES] * xp_scr[r0 + 8:r0 + 8 + CHUNK, l0:l0 + LANES]
            for k in range(CONV_WIDTH - 1):
                off = r0 + 8 - (CONV_WIDTH - 1) + k
                acc = acc + w_ref[k:k + 1, l0:l0 + LANES] * xp_scr[off:off + CHUNK, l0:l0 + LANES]
            if apply_silu:
                acc = _silu(acc)
            out_scr[r0:r0 + CHUNK, l0:l0 + LANES] = acc


CAST_ROWS = 512


def _cast_kernel(a_ref, o_ref):
    o_ref[...] = a_ref[...].astype(BF16)


def _cast_bf16(a):
    r, c = a.shape
    return pl.pallas_call(
        _cast_kernel,
        out_shape=jax.ShapeDtypeStruct((r, c), BF16),
        grid=(r // CAST_ROWS,),
        in_specs=[pl.BlockSpec((CAST_ROWS, c), lambda i: (i, 0))],
        out_specs=pl.BlockSpec((CAST_ROWS, c), lambda i: (i, 0)),
        compiler_params=_params(1),
        name="cast_bf16",
    )(a)


WPREP_ROWS = 2048
WPREP_COLS = 512
WPREP_HEAD_BLOCKS = N_HEAD // WPREP_COLS
WPREP_BLOCKS = N_MAIN // WPREP_COLS
WPREP_TILES = WPREP_COLS // LANES


def _w_in_prep_kernel(a_ref, nxt_ref, o_ref):
    j = pl.program_id(2)

    @pl.when(j < WPREP_HEAD_BLOCKS)
    def _():
        o_ref[...] = a_ref[...].astype(BF16)

    @pl.when(j >= WPREP_HEAD_BLOCKS)
    def _():
        low = _iota((WPREP_ROWS, LANES), 1) < HALF_LANES
        for m in range(WPREP_TILES):
            t0 = a_ref[:, m * LANES:(m + 1) * LANES]
            t1 = a_ref[:, (m + 1) * LANES:(m + 2) * LANES] if m + 1 < WPREP_TILES else nxt_ref[...]
            shifted = jnp.where(low, pltpu.roll(t0, HALF_LANES, 1), pltpu.roll(t1, HALF_LANES, 1))
            o_ref[:, m * LANES:(m + 1) * LANES] = shifted.astype(BF16)


def _w_in_prep(w_in):
    assert SSD_HEADS == HALF_LANES
    depth, d, _ = w_in.shape
    first_nxt = (WPREP_HEAD_BLOCKS + 1) * WPREP_TILES
    return pl.pallas_call(
        _w_in_prep_kernel,
        out_shape=jax.ShapeDtypeStruct((depth, d, N_MAIN), BF16),
        grid=(depth, d // WPREP_ROWS, WPREP_BLOCKS),
        in_specs=[pl.BlockSpec((None, WPREP_ROWS, WPREP_COLS), lambda l, r, j: (l, r, j)),
                  pl.BlockSpec((None, WPREP_ROWS, LANES),
                               lambda l, r, j: (l, r, jnp.maximum((j + 1) * WPREP_TILES, first_nxt)))],
        out_specs=pl.BlockSpec((None, WPREP_ROWS, WPREP_COLS), lambda l, r, j: (l, r, j)),
        compiler_params=_params(3),
        name="w_in_prep",
    )(w_in, w_in)


def _mm_kernel(a_ref, b_ref, o_ref):
    o_ref[...] = _dot(a_ref[...], b_ref[...])


def _matmul(a, b_all, layer, tm, tn):
    m, k = a.shape
    n = b_all.shape[2]
    return pl.pallas_call(
        _mm_kernel,
        out_shape=jax.ShapeDtypeStruct((m, n), F32),
        grid=(m // tm, n // tn),
        in_specs=[pl.BlockSpec((tm, k), lambda i, j: (i, 0)),
                  pl.BlockSpec((None, k, tn), lambda i, j: (layer, 0, j))],
        out_specs=pl.BlockSpec((tm, tn), lambda i, j: (i, j)),
        compiler_params=_params(2),
        name="in_proj",
    )(a, b_all)


def _head_prep_kernel(sm_ref, bias_ref, alog_ref, cum_ref, pd_ref, q1_ref, q2_ref, cum_t_ref, pd_t_ref):
    ltri = (_iota((CHUNK, CHUNK), 1) <= _iota((CHUNK, CHUNK), 0)).astype(F32)
    lane = _iota((1, LANES), 1)
    is_ssd = lane < SSD_HEADS
    a_neg = jnp.where(is_ssd, -jnp.exp(alog_ref[...]), 0.0)
    for c in range(CHUNKS_PER_BLOCK):
        r0 = c * CHUNK
        pre = sm_ref[r0:r0 + CHUNK, :] + bias_ref[...]
        dt = _softplus(pre)
        log_f = -_softplus(-pre)
        cum = _dot_f32(ltri, jnp.where(is_ssd, dt * a_neg, log_f))
        pd = jnp.where(is_ssd, dt, pre)
        cum_ref[r0:r0 + CHUNK, :] = cum
        pd_ref[r0:r0 + CHUNK, :] = pd
        q1_ref[r0:r0 + CHUNK, :] = jnp.exp(cum[CHUNK - 1:CHUNK, :] - cum) * dt
        q2_ref[r0:r0 + CHUNK, :] = jnp.exp(cum)
        cum_t_ref[:, r0:r0 + CHUNK] = cum.T
        pd_t_ref[:, r0:r0 + CHUNK] = pd.T


def _head_prep(small, bias_all, alog_all, layer):
    t = small.shape[0]
    tb = TIME_BLOCK
    nb = t // tb
    row_blk = pl.BlockSpec((tb, LANES), lambda i: (i, 0))
    vec = pl.BlockSpec((None, 1, LANES), lambda i: (layer, 0, 0))
    t_blk = pl.BlockSpec((None, LANES, tb), lambda i: (i, 0, 0))
    flat = jax.ShapeDtypeStruct((t, LANES), F32)
    transposed = jax.ShapeDtypeStruct((nb, LANES, tb), F32)
    return pl.pallas_call(
        _head_prep_kernel,
        out_shape=(flat, flat, flat, flat, transposed, transposed),
        grid=(nb,),
        in_specs=[row_blk, vec, vec],
        out_specs=(row_blk, row_blk, row_blk, row_blk, t_blk, t_blk),
        compiler_params=_params(1),
        name="head_prep",
    )(small, bias_all, alog_all)


def _ssd_kernel(z_ref, xs_ref, bm_ref, cm_ref, cum_ref, q1_ref, q2_ref, cum_t_ref, dt_t_ref,
                cwx_ref, cwb_ref, cwc_ref, cbx_ref, cbb_ref, cbc_ref, dexp_ref, nw_ref,
                y_ref,
                xpx_scr, xpb_scr, xpc_scr, xcx_scr, xcb_scr, xcc_scr, st_scr):
    g = pl.program_id(1)
    first = pl.program_id(2) == 0
    gw = SSD_GROUP_WIDTH

    @pl.when(first)
    def _():
        st_scr[...] = jnp.zeros(st_scr.shape, F32)

    _conv_block(xs_ref, xpx_scr, cwx_ref, cbx_ref, xcx_scr, first, gw, True)
    _conv_block(bm_ref, xpb_scr, cwb_ref, cbb_ref, xcb_scr, first, SSD_STATE, True)
    _conv_block(cm_ref, xpc_scr, cwc_ref, cbc_ref, xcc_scr, first, SSD_STATE, True)

    causal = _iota((CHUNK, CHUNK), 1) <= _iota((CHUNK, CHUNK), 0)
    low_half = _iota((CHUNK, LANES), 1) < SSD_HEAD_DIM
    expand = ((jnp.right_shift(_iota((LANES, gw), 1), SSD_HEAD_SHIFT) + g * SSD_HPG)
              == _iota((LANES, gw), 0)).astype(F32).astype(BF16)

    for c in range(CHUNKS_PER_BLOCK):
        r0 = c * CHUNK
        xs = xcx_scr[r0:r0 + CHUNK, :]
        bm = xcb_scr[r0:r0 + CHUNK, :]
        cm16 = xcc_scr[r0:r0 + CHUNK, :].astype(BF16)
        acum_all = cum_ref[r0:r0 + CHUNK, :]
        acum_t = cum_t_ref[:, r0:r0 + CHUNK]
        dt_t = dt_t_ref[:, r0:r0 + CHUNK]

        cb = _dot_nt(cm16, bm.astype(BF16))

        ydiag = []
        for j in range(SSD_HPG // 2):
            wts = []
            for r in (2 * j, 2 * j + 1):
                acol = _column(acum_all, g * SSD_HPG + r)
                seg = acol - acum_t[r:r + 1, :]
                decay = jnp.exp(jnp.where(causal, seg, -jnp.inf))
                wts.append((cb * decay * dt_t[r:r + 1, :]).astype(BF16))
            lhs = jnp.concatenate(wts, axis=1)
            xp = xs[:, j * LANES:(j + 1) * LANES]
            rhs = jnp.concatenate([jnp.where(low_half, xp, 0.0).astype(BF16),
                                   jnp.where(low_half, 0.0, xp).astype(BF16)], axis=0)
            ydiag.append(_dot(lhs, rhs))
        y = jnp.concatenate(ydiag, axis=1)

        q1 = q1_ref[r0:r0 + CHUNK, :]
        q2 = q2_ref[r0:r0 + CHUNK, :]
        q1h = q1.astype(BF16)
        q2h = q2.astype(BF16)
        q1l = (q1 - q1h.astype(F32)).astype(BF16)
        q2l = (q2 - q2h.astype(F32)).astype(BF16)
        qe = _dot(jnp.concatenate([q1h, q1l, q2h, q2l], axis=0), expand)
        q1e = qe[0:CHUNK] + qe[CHUNK:2 * CHUNK]
        q2e = qe[2 * CHUNK:3 * CHUNK] + qe[3 * CHUNK:4 * CHUNK]

        prev = st_scr[...]
        states = _dot(bm.T.astype(BF16), (q1e * xs).astype(BF16))
        y = y + _dot(cm16, prev.astype(BF16)) * q2e
        st_scr[...] = prev * q2e[CHUNK - 1:CHUNK, :] + states

        y = y + dexp_ref[...] * xs
        y = y * _silu(z_ref[r0:r0 + CHUNK, :])
        y = y * lax.rsqrt(jnp.mean(y * y, axis=1, keepdims=True) + 1e-6)
        y_ref[r0:r0 + CHUNK, :] = (y * nw_ref[...]).astype(BF16)


def _ssd_mixer(proj, cum, q1, q2, cum_t, pd_t, batch, seq, layer, conv_w, conv_b, d_exp, norm_w):
    t = proj.shape[0]
    nb = seq // TIME_BLOCK
    tb = TIME_BLOCK
    gw = SSD_GROUP_WIDTH

    def col_spec(width, col0):
        base = col0 // width
        return pl.BlockSpec((tb, width), lambda b, g, i: (b * nb + i, base + g))

    def w_spec(nrows, width, col0):
        base = col0 // width
        return pl.BlockSpec((None, nrows, width), lambda b, g, i: (layer, 0, base + g))

    seq_blk = pl.BlockSpec((tb, LANES), lambda b, g, i: (b * nb + i, 0))
    t_blk = pl.BlockSpec((None, SSD_HPG, tb), lambda b, g, i: (b * nb + i, g, 0))
    cx, cb_, cc = 0, SSD_WIDTH, SSD_WIDTH + SSD_GROUPS * SSD_STATE
    return pl.pallas_call(
        _ssd_kernel,
        out_shape=jax.ShapeDtypeStruct((t, SSD_WIDTH), BF16),
        grid=(batch, SSD_GROUPS, nb),
        in_specs=[col_spec(gw, COL_Z), col_spec(gw, COL_XS), col_spec(SSD_STATE, COL_B),
                  col_spec(SSD_STATE, COL_C),
                  seq_blk, seq_blk, seq_blk, t_blk, t_blk,
                  w_spec(CONV_WIDTH, gw, cx), w_spec(CONV_WIDTH, SSD_STATE, cb_),
                  w_spec(CONV_WIDTH, SSD_STATE, cc),
                  w_spec(1, gw, cx), w_spec(1, SSD_STATE, cb_), w_spec(1, SSD_STATE, cc),
                  w_spec(1, gw, 0), w_spec(1, gw, 0)],
        out_specs=pl.BlockSpec((tb, gw), lambda b, g, i: (b * nb + i, g)),
        scratch_shapes=[pltpu.VMEM((tb + 2 * SUBLANES, gw), F32),
                        pltpu.VMEM((tb + 2 * SUBLANES, SSD_STATE), F32),
                        pltpu.VMEM((tb + 2 * SUBLANES, SSD_STATE), F32),
                        pltpu.VMEM((tb, gw), F32),
                        pltpu.VMEM((tb, SSD_STATE), F32),
                        pltpu.VMEM((tb, SSD_STATE), F32),
                        pltpu.VMEM((SSD_STATE, gw), F32)],
        compiler_params=_params(3),
        name="ssd_mixer",
    )(proj, proj, proj, proj, cum, q1, q2, cum_t, pd_t,
      conv_w, conv_w, conv_w, conv_b, conv_b, conv_b, d_exp, norm_w)


def _lru_kernel(x_ref, gate_ref, cw_ref, cb_ref, wa_ref, wx_ref, ba_ref, bx_ref, lam_ref,
                y_ref, xp_scr, xc_scr, h_scr):
    first = pl.program_id(2) == 0

    @pl.when(first)
    def _():
        h_scr[...] = jnp.zeros(h_scr.shape, F32)

    _conv_block(x_ref, xp_scr, cw_ref, cb_ref, xc_scr, first, LRU_BLOCK_DIM, False)
    row_in_tile = jnp.bitwise_and(_iota((CHUNK, LANES), 0), SUBLANES - 1)
    sp_lam = _softplus(-lam_ref[...])
    carry = h_scr[...]

    for c in range(CHUNKS_PER_BLOCK):
        r0 = c * CHUNK
        xc = xc_scr[r0:r0 + CHUNK, :]
        xc16 = xc.astype(BF16)
        r = _sigmoid(_dot(xc16, wa_ref[0]) + ba_ref[...])
        i = _sigmoid(_dot(xc16, wx_ref[0]) + bx_ref[...])
        log_a = -LRU_C * r * sp_lam
        a = jnp.exp(log_a)
        u = jnp.sqrt(-jnp.tanh(log_a) * (a * a + 1.0)) * (i * xc)
        d = 1
        while d < SUBLANES:
            keep = row_in_tile >= d
            a_sh = pltpu.roll(a, d, 0)
            u_sh = pltpu.roll(u, d, 0)
            u = jnp.where(keep, u + a * u_sh, u)
            a = jnp.where(keep, a * a_sh, a)
            d *= 2
        hs = []
        for t in range(CHUNK // SUBLANES):
            s0 = t * SUBLANES
            hs.append(u[s0:s0 + SUBLANES, :] + a[s0:s0 + SUBLANES, :] * carry)
            carry = hs[-1][SUBLANES - 1:SUBLANES, :]
        h = jnp.concatenate(hs, axis=0)
        y_ref[r0:r0 + CHUNK, :] = (h * _silu(gate_ref[r0:r0 + CHUNK, :])).astype(BF16)

    h_scr[...] = carry


def _lru_mixer(proj, batch, seq, layer, conv_w, conv_b, w_a, w_x, b_a, b_x, lam):
    t = proj.shape[0]
    nb = seq // TIME_BLOCK
    tb = TIME_BLOCK
    bd = LRU_BLOCK_DIM

    def col_spec(col0):
        base = col0 // bd
        return pl.BlockSpec((tb, bd), lambda b, h, i: (b * nb + i, base + h))

    vec = pl.BlockSpec((None, 1, bd), lambda b, h, i: (layer, 0, h))
    wspec = pl.BlockSpec((None, 1, bd, bd), lambda b, h, i: (layer, h, 0, 0))
    return pl.pallas_call(
        _lru_kernel,
        out_shape=jax.ShapeDtypeStruct((t, LRU_WIDTH), BF16),
        grid=(batch, LRU_BLOCKS, nb),
        in_specs=[col_spec(COL_LRU_X), col_spec(COL_LRU_G),
                  pl.BlockSpec((None, CONV_WIDTH, bd), lambda b, h, i: (layer, 0, h)), vec,
                  wspec, wspec, vec, vec, vec],
        out_specs=pl.BlockSpec((tb, bd), lambda b, h, i: (b * nb + i, h)),
        scratch_shapes=[pltpu.VMEM((tb + 2 * SUBLANES, bd), F32),
                        pltpu.VMEM((tb, bd), F32),
                        pltpu.VMEM((1, bd), F32)],
        compiler_params=_params(3),
        name="lru_mixer",
    )(proj, proj, conv_w, conv_b, w_a, w_x, b_a, b_x, lam)


def _mlstm_kernel(x_ref, o_ref, gate_ref, cum_ref, pd_ref, cum_t_ref, pd_t_ref, cw_ref, cb_ref,
                  wq_ref, wk_ref, wv_ref, nw_ref,
                  y_ref, xp_scr, xc_scr, c_scr, n_scr, m_scr):
    h = pl.program_id(1)
    first = pl.program_id(2) == 0
    dh = MLSTM_HEAD_DIM

    @pl.when(first)
    def _():
        c_scr[...] = jnp.zeros(c_scr.shape, F32)
        n_scr[...] = jnp.zeros(n_scr.shape, F32)
        m_scr[...] = jnp.zeros(m_scr.shape, F32)

    _conv_block(x_ref, xp_scr, cw_ref, cb_ref, xc_scr, first, dh, True)
    causal = _iota((CHUNK, CHUNK), 1) <= _iota((CHUNK, CHUNK), 0)
    is_head = _iota((MLSTM_HEADS, CHUNK), 0) == h

    for c in range(CHUNKS_PER_BLOCK):
        r0 = c * CHUNK
        xm16 = x_ref[r0:r0 + CHUNK, :].astype(BF16)
        xc16 = xc_scr[r0:r0 + CHUNK, :].astype(BF16)
        q = _dot(xc16, wq_ref[0])
        k = _dot(xc16, wk_ref[0]) * (dh ** -0.5)
        v = _dot(xm16, wv_ref[0])
        q16 = q.astype(BF16)
        v16 = v.astype(BF16)

        i_row = jnp.sum(jnp.where(is_head, pd_t_ref[:, r0:r0 + CHUNK], 0.0), axis=0, keepdims=True)
        b_row = jnp.sum(jnp.where(is_head, cum_t_ref[:, r0:r0 + CHUNK], 0.0), axis=0, keepdims=True)
        i_col = _column(pd_ref[r0:r0 + CHUNK, :], SMALL_I + h)
        b_col = _column(cum_ref[r0:r0 + CHUNK, :], SMALL_F + h)

        m_prev = m_scr[...]
        a_inter = b_col + m_prev
        d_intra = jnp.where(causal, b_col - b_row + i_row, -jnp.inf)
        m_t = jnp.maximum(a_inter, jnp.max(d_intra, axis=1, keepdims=True))
        w_intra = jnp.exp(d_intra - m_t)
        w_inter = jnp.exp(a_inter - m_t)
        s = _dot_nt(q16, k.astype(BF16)) * w_intra
        c_st = c_scr[...]
        n_st = n_scr[...]
        num = _dot(s.astype(BF16), v16) + w_inter * _dot(q16, c_st.astype(BF16))
        den = jnp.sum(s, axis=1, keepdims=True) + w_inter * jnp.sum(q * n_st, axis=1, keepdims=True)
        hs = num / jnp.maximum(jnp.abs(den), jnp.exp(-m_t))

        m_new = m_t[CHUNK - 1:CHUNK, :]
        b_last = b_col[CHUNK - 1:CHUNK, :]
        w_s = jnp.exp(b_last - b_col + i_col - m_new)
        carry_decay = jnp.exp(b_last + m_prev - m_new)
        kw = k * w_s
        c_scr[...] = carry_decay * c_st + _dot(kw.T.astype(BF16), v16)
        n_scr[...] = carry_decay * n_st + jnp.sum(kw, axis=0, keepdims=True)
        m_scr[...] = m_new

        hn = hs * lax.rsqrt(jnp.mean(hs * hs, axis=1, keepdims=True) + 1e-6) * nw_ref[...]
        out = hn * _sigmoid(o_ref[r0:r0 + CHUNK, :]) * _silu(gate_ref[r0:r0 + CHUNK, :])
        y_ref[r0:r0 + CHUNK, :] = out.astype(BF16)


def _mlstm_mixer(proj, cum, pd, cum_t, pd_t, batch, seq, layer, conv_w, conv_b, w_q, w_k, w_v, norm_w):
    t = proj.shape[0]
    nb = seq // TIME_BLOCK
    tb = TIME_BLOCK
    dh = MLSTM_HEAD_DIM

    def col_spec(col0):
        base = col0 // dh
        return pl.BlockSpec((tb, dh), lambda b, h, i: (b * nb + i, base + h))

    def t_blk(row0):
        return pl.BlockSpec((None, MLSTM_HEADS, tb), lambda b, h, i: (b * nb + i, row0 // MLSTM_HEADS, 0))

    seq_blk = pl.BlockSpec((tb, LANES), lambda b, h, i: (b * nb + i, 0))
    vec = pl.BlockSpec((None, 1, dh), lambda b, h, i: (layer, 0, h))
    wspec = pl.BlockSpec((None, 1, dh, dh), lambda b, h, i: (layer, h, 0, 0))
    return pl.pallas_call(
        _mlstm_kernel,
        out_shape=jax.ShapeDtypeStruct((t, MLSTM_WIDTH), BF16),
        grid=(batch, MLSTM_HEADS, nb),
        in_specs=[col_spec(COL_ML_X), col_spec(COL_ML_O), col_spec(COL_ML_G),
                  seq_blk, seq_blk, t_blk(SMALL_F), t_blk(SMALL_I),
                  pl.BlockSpec((None, CONV_WIDTH, dh), lambda b, h, i: (layer, 0, h)), vec,
                  wspec, wspec, wspec, vec],
        out_specs=pl.BlockSpec((tb, dh), lambda b, h, i: (b * nb + i, h)),
        scratch_shapes=[pltpu.VMEM((tb + 2 * SUBLANES, dh), F32),
                        pltpu.VMEM((tb, dh), F32),
                        pltpu.VMEM((dh, dh), F32),
                        pltpu.VMEM((1, dh), F32),
                        pltpu.VMEM((1, 1), F32)],
        compiler_params=_params(3),
        name="mlstm_mixer",
    )(proj, proj, proj, cum, pd, cum_t, pd_t, conv_w, conv_b, w_q, w_k, w_v, norm_w)


OUT_TM = 512
OUT_TK = 512
OUT_K_SSD = SSD_WIDTH // OUT_TK
OUT_K_LRU = LRU_WIDTH // OUT_TK
OUT_K_ML = MLSTM_WIDTH // OUT_TK
OUT_K_STEPS = OUT_K_SSD + OUT_K_LRU + OUT_K_ML
OUT_X_ROWS = OUT_TM // OUT_K_STEPS
LN_ROWS = 8


def _out_ln_kernel(ys_ref, yl_ref, ym_ref, w_ref, x_ref, lnw_ref, lnb_ref, of_ref, ob_ref):
    k = pl.program_id(1)

    @pl.when(k == 0)
    def _():
        of_ref[...] = _dot(ys_ref[...], w_ref[...])

    @pl.when(jnp.logical_and(k > 0, k < OUT_K_SSD))
    def _():
        of_ref[...] += _dot(ys_ref[...], w_ref[...])

    @pl.when(jnp.logical_and(k >= OUT_K_SSD, k < OUT_K_SSD + OUT_K_LRU))
    def _():
        of_ref[...] += _dot(yl_ref[...], w_ref[...])

    @pl.when(k >= OUT_K_SSD + OUT_K_LRU)
    def _():
        of_ref[...] += _dot(ym_ref[...], w_ref[...])

    xr = pl.ds(pl.multiple_of(k * OUT_X_ROWS, OUT_X_ROWS), OUT_X_ROWS)
    of_ref[xr, :] += DEEPNORM_ALPHA * x_ref[...]

    @pl.when(k == OUT_K_STEPS - 1)
    def _():
        def ln(r, carry):
            r0 = pl.multiple_of(r * LN_ROWS, LN_ROWS)
            v = of_ref[pl.ds(r0, LN_ROWS), :]
            mu = jnp.mean(v, axis=1, keepdims=True)
            vc = v - mu
            var = jnp.mean(vc * vc, axis=1, keepdims=True)
            o = vc * lax.rsqrt(var + 1e-5) * lnw_ref[...] + lnb_ref[...]
            of_ref[pl.ds(r0, LN_ROWS), :] = o
            ob_ref[pl.ds(r0, LN_ROWS), :] = o.astype(BF16)
            return carry

        lax.fori_loop(0, OUT_TM // LN_ROWS, ln, 0, unroll=2)


def _out_ln(y_ssd, y_lru, y_ml, w_out16, x, layer, ln_w, ln_b):
    t = x.shape[0]
    tm, tk = OUT_TM, OUT_TK
    k1, k2 = OUT_K_SSD, OUT_K_SSD + OUT_K_LRU
    vec = pl.BlockSpec((None, 1, D_MODEL), lambda i, k: (layer, 0, 0))
    return pl.pallas_call(
        _out_ln_kernel,
        out_shape=(jax.ShapeDtypeStruct((t, D_MODEL), F32), jax.ShapeDtypeStruct((t, D_MODEL), BF16)),
        grid=(t // tm, OUT_K_STEPS),
        in_specs=[pl.BlockSpec((tm, tk), lambda i, k: (i, jnp.minimum(k, k1 - 1))),
                  pl.BlockSpec((tm, tk), lambda i, k: (i, jnp.clip(k - k1, 0, OUT_K_LRU - 1))),
                  pl.BlockSpec((tm, tk), lambda i, k: (i, jnp.clip(k - k2, 0, OUT_K_ML - 1))),
                  pl.BlockSpec((None, tk, D_MODEL), lambda i, k: (layer, k, 0)),
                  pl.BlockSpec((OUT_X_ROWS, D_MODEL), lambda i, k: (i * OUT_K_STEPS + k, 0)),
                  vec, vec],
        out_specs=(pl.BlockSpec((tm, D_MODEL), lambda i, k: (i, 0)),
                   pl.BlockSpec((tm, D_MODEL), lambda i, k: (i, 0))),
        compiler_params=_params(2),
        name="out_proj_ln",
    )(y_ssd, y_lru, y_ml, w_out16, x, ln_w, ln_b)


def _row3(p):
    return p.reshape(p.shape[0], 1, p.shape[1])


def kernel(x, w_in, ssd_conv_w, ssd_conv_b, ssd_dt_bias, ssd_a_log, ssd_d, ssd_norm_w, lru_conv_w, lru_conv_b,
           lru_w_a, lru_b_a, lru_w_x, lru_b_x, lru_lambda, mlstm_conv_w, mlstm_conv_b, mlstm_w_q, mlstm_w_k,
           mlstm_w_v, mlstm_b_i, mlstm_b_f, mlstm_norm_w, w_out, ln_w, ln_b):
    batch, seq, d = x.shape
    depth = w_in.shape[0]
    assert d == D_MODEL and seq % TIME_BLOCK == 0 and depth == DEPTH and w_in.shape[2] == N_IN
    xf = x.reshape(batch * seq, d)
    x16 = _cast_bf16(xf)

    w_main = _w_in_prep(w_in)
    n_pad = LANES - SSD_HEADS - 2 * MLSTM_HEADS
    w_small = jnp.concatenate([w_in[:, :, N_HEAD:N_HEAD + SSD_HEADS], w_in[:, :, N_IN - 2 * MLSTM_HEADS:],
                               jnp.zeros((depth, d, n_pad), F32)], axis=2).astype(BF16)
    w_out16 = _cast_bf16(w_out.reshape(depth * MIX_WIDTH, d)).reshape(depth, MIX_WIDTH, d)
    small_bias = _row3(jnp.concatenate([ssd_dt_bias, mlstm_b_i, mlstm_b_f, jnp.zeros((depth, n_pad), F32)], axis=1))
    alog_row = _row3(jnp.concatenate([ssd_a_log, jnp.zeros((depth, LANES - SSD_HEADS), F32)], axis=1))
    d_exp = _row3(jnp.repeat(ssd_d, SSD_HEAD_DIM, axis=1))
    lru_wa16, lru_wx16 = lru_w_a.astype(BF16), lru_w_x.astype(BF16)
    wq16, wk16, wv16 = mlstm_w_q.astype(BF16), mlstm_w_k.astype(BF16), mlstm_w_v.astype(BF16)

    for l in range(depth):
        proj = _matmul(x16, w_main, l, 1024, 512)
        small = _matmul(x16, w_small, l, 1024, LANES)
        cum, pd, q1, q2, cum_t, pd_t = _head_prep(small, small_bias, alog_row, l)
        y_ssd = _ssd_mixer(proj, cum, q1, q2, cum_t, pd_t, batch, seq, l, ssd_conv_w, _row3(ssd_conv_b),
                           d_exp, _row3(ssd_norm_w))
        y_lru = _lru_mixer(proj, batch, seq, l, lru_conv_w, _row3(lru_conv_b), lru_wa16, lru_wx16,
                           _row3(lru_b_a), _row3(lru_b_x), _row3(lru_lambda))
        y_ml = _mlstm_mixer(proj, cum, pd, cum_t, pd_t, batch, seq, l, mlstm_conv_w, _row3(mlstm_conv_b),
                            wq16, wk16, wv16, _row3(mlstm_norm_w))
        xf, x16 = _out_ln(y_ssd, y_lru, y_ml, w_out16, xf, l, _row3(ln_w), _row3(ln_b))
    return xf.reshape(batch, seq, d)
```

```python
import jax
import jax.numpy as jnp
from jax import lax
from jax.experimental import pallas as pl
from jax.experimental.pallas import tpu as pltpu

F32 = jnp.float32
BF16 = jnp.bfloat16

D_MODEL = 4096
DEPTH = 2
MIX_WIDTH = 2 * D_MODEL
SSD_WIDTH = MIX_WIDTH // 2
LRU_WIDTH = MIX_WIDTH // 4
MLSTM_WIDTH = MIX_WIDTH - SSD_WIDTH - LRU_WIDTH
SSD_HEAD_DIM = 64
SSD_HEAD_SHIFT = 6
SSD_HEADS = SSD_WIDTH // SSD_HEAD_DIM
SSD_GROUPS = 8
SSD_HPG = SSD_HEADS // SSD_GROUPS
SSD_STATE = 128
SSD_GROUP_WIDTH = SSD_WIDTH // SSD_GROUPS
LRU_BLOCKS = 16
LRU_BLOCK_DIM = LRU_WIDTH // LRU_BLOCKS
LRU_C = 8.0
MLSTM_HEADS = 8
MLSTM_HEAD_DIM = MLSTM_WIDTH // MLSTM_HEADS
CONV_WIDTH = 4
CHUNK = 128
DEEPNORM_ALPHA = (2.0 * DEPTH) ** 0.25

SUBLANES = 8
LANES = 128
HALF_LANES = LANES // 2

XBC_WIDTH = SSD_WIDTH + 2 * SSD_GROUPS * SSD_STATE
N_HEAD = SSD_WIDTH + XBC_WIDTH
N_TAIL = 2 * LRU_WIDTH + 3 * MLSTM_WIDTH
N_MAIN = N_HEAD + N_TAIL
N_IN = N_HEAD + SSD_HEADS + N_TAIL + 2 * MLSTM_HEADS
COL_Z = 0
COL_XS = SSD_WIDTH
COL_B = COL_XS + SSD_WIDTH
COL_C = COL_B + SSD_GROUPS * SSD_STATE
COL_LRU_X = COL_C + SSD_GROUPS * SSD_STATE
COL_LRU_G = COL_LRU_X + LRU_WIDTH
COL_ML_X = COL_LRU_G + LRU_WIDTH
COL_ML_O = COL_ML_X + MLSTM_WIDTH
COL_ML_G = COL_ML_O + MLSTM_WIDTH
SMALL_I = SSD_HEADS
SMALL_F = SSD_HEADS + MLSTM_HEADS

VMEM_LIMIT = 56 * 1024 * 1024
TIME_BLOCK = 1024
CHUNKS_PER_BLOCK = TIME_BLOCK // CHUNK


def _params(n_axes):
    return pltpu.CompilerParams(dimension_semantics=("arbitrary",) * n_axes, vmem_limit_bytes=VMEM_LIMIT)


def _sigmoid(x):
    return 0.5 * jnp.tanh(0.5 * x) + 0.5


def _silu(x):
    return x * _sigmoid(x)


def _softplus(x):
    return jnp.maximum(x, 0.0) + jnp.log1p(jnp.exp(-jnp.abs(x)))


def _dot(a, b):
    return jnp.dot(a, b, preferred_element_type=F32)


def _dot_nt(a, b):
    return lax.dot_general(a, b, (((1,), (1,)), ((), ())), preferred_element_type=F32)


def _dot_f32(a, b):
    return jnp.dot(a, b, precision=lax.Precision.HIGHEST, preferred_element_type=F32)


def _iota(shape, dim):
    return lax.broadcasted_iota(jnp.int32, shape, dim)


def _column(x, idx):
    lane = _iota(x.shape, 1)
    return jnp.sum(jnp.where(lane == idx, x, 0.0), axis=1, keepdims=True)


def _conv_block(x_ref, xp_scr, w_ref, b_ref, out_scr, first, width, apply_silu):
    tb = TIME_BLOCK

    @pl.when(first)
    def _():
        xp_scr[0:SUBLANES, :] = jnp.zeros((SUBLANES, width), F32)

    @pl.when(jnp.logical_not(first))
    def _():
        xp_scr[0:SUBLANES, :] = xp_scr[tb:tb + SUBLANES, :]

    xp_scr[SUBLANES:SUBLANES + tb, :] = x_ref[...]
    for c in range(CHUNKS_PER_BLOCK):
        r0 = c * CHUNK
        for l0 in range(0, width, LANES):
            acc = b_ref[:, l0:l0 + LANES] + w_ref[3:4, l0:l0 + LANES] * xp_scr[r0 + 8:r0 + 8 + CHUNK, l0:l0 + LANES]
            for k in range(CONV_WIDTH - 1):
                off = r0 + 8 - (CONV_WIDTH - 1) + k
                acc = acc + w_ref[k:k + 1, l0:l0 + LANES] * xp_scr[off:off + CHUNK, l0:l0 + LANES]
            if apply_silu:
                acc = _silu(acc)
            out_scr[r0:r0 + CHUNK, l0:l0 + LANES] = acc


CAST_ROWS = 512


def _cast_kernel(a_ref, o_ref):
    o_ref[...] = a_ref[...].astype(BF16)


def _cast_bf16(a):
    r, c = a.shape
    return pl.pallas_call(
        _cast_kernel,
        out_shape=jax.ShapeDtypeStruct((r, c), BF16),
        grid=(r // CAST_ROWS,),
        in_specs=[pl.BlockSpec((CAST_ROWS, c), lambda i: (i, 0))],
        out_specs=pl.BlockSpec((CAST_ROWS, c), lambda i: (i, 0)),
        compiler_params=_params(1),
        name="cast_bf16",
    )(a)


IN_TM = 2048
IN_TN = 512
IN_SMALL_TM = 1024


def _in_proj_kernel(a_ref, wt_ref, o_ref):
    o_ref[...] = _dot_nt(a_ref[...], wt_ref[...].astype(BF16))


def _in_proj(x16, wt_all, layer):
    m, k = x16.shape
    head_blocks = N_HEAD // IN_TN

    def w_index(i, j):
        row = j * IN_TN + jnp.where(j >= head_blocks, SSD_HEADS, 0)
        return (layer, pl.multiple_of(row, SSD_HEADS), 0)

    return pl.pallas_call(
        _in_proj_kernel,
        out_shape=jax.ShapeDtypeStruct((m, N_MAIN), F32),
        grid=(m // IN_TM, N_MAIN // IN_TN),
        in_specs=[pl.BlockSpec((IN_TM, k), lambda i, j: (i, 0), pipeline_mode=pl.Buffered(1)),
                  pl.BlockSpec((None, pl.Element(IN_TN), pl.Element(k)), w_index)],
        out_specs=pl.BlockSpec((IN_TM, IN_TN), lambda i, j: (i, j)),
        compiler_params=_params(2),
        name="in_proj",
    )(x16, wt_all)


def _in_proj_small(x16, wt_small, layer):
    m, k = x16.shape
    return pl.pallas_call(
        _in_proj_kernel,
        out_shape=jax.ShapeDtypeStruct((m, LANES), F32),
        grid=(m // IN_SMALL_TM,),
        in_specs=[pl.BlockSpec((IN_SMALL_TM, k), lambda i: (i, 0)),
                  pl.BlockSpec((None, LANES, k), lambda i: (layer, 0, 0))],
        out_specs=pl.BlockSpec((IN_SMALL_TM, LANES), lambda i: (i, 0)),
        compiler_params=_params(1),
        name="in_proj_small",
    )(x16, wt_small)


def _head_prep_kernel(sm_ref, bias_ref, alog_ref, cum_ref, pd_ref, q1_ref, q2_ref, cum_t_ref, pd_t_ref):
    ltri = (_iota((CHUNK, CHUNK), 1) <= _iota((CHUNK, CHUNK), 0)).astype(F32)
    lane = _iota((1, LANES), 1)
    is_ssd = lane < SSD_HEADS
    a_neg = jnp.where(is_ssd, -jnp.exp(alog_ref[...]), 0.0)
    for c in range(CHUNKS_PER_BLOCK):
        r0 = c * CHUNK
        pre = sm_ref[r0:r0 + CHUNK, :] + bias_ref[...]
        dt = _softplus(pre)
        log_f = -_softplus(-pre)
        cum = _dot_f32(ltri, jnp.where(is_ssd, dt * a_neg, log_f))
        pd = jnp.where(is_ssd, dt, pre)
        cum_ref[r0:r0 + CHUNK, :] = cum
        pd_ref[r0:r0 + CHUNK, :] = pd
        q1_ref[r0:r0 + CHUNK, :] = jnp.exp(cum[CHUNK - 1:CHUNK, :] - cum) * dt
        q2_ref[r0:r0 + CHUNK, :] = jnp.exp(cum)
        cum_t_ref[:, r0:r0 + CHUNK] = cum.T
        pd_t_ref[:, r0:r0 + CHUNK] = pd.T


def _head_prep(small, bias_all, alog_all, layer):
    t = small.shape[0]
    tb = TIME_BLOCK
    nb = t // tb
    row_blk = pl.BlockSpec((tb, LANES), lambda i: (i, 0))
    vec = pl.BlockSpec((None, 1, LANES), lambda i: (layer, 0, 0))
    t_blk = pl.BlockSpec((None, LANES, tb), lambda i: (i, 0, 0))
    flat = jax.ShapeDtypeStruct((t, LANES), F32)
    transposed = jax.ShapeDtypeStruct((nb, LANES, tb), F32)
    return pl.pallas_call(
        _head_prep_kernel,
        out_shape=(flat, flat, flat, flat, transposed, transposed),
        grid=(nb,),
        in_specs=[row_blk, vec, vec],
        out_specs=(row_blk, row_blk, row_blk, row_blk, t_blk, t_blk),
        compiler_params=_params(1),
        name="head_prep",
    )(small, bias_all, alog_all)


def _ssd_kernel(z_ref, xs_ref, bm_ref, cm_ref, cum_ref, q1_ref, q2_ref, cum_t_ref, dt_t_ref,
                cwx_ref, cwb_ref, cwc_ref, cbx_ref, cbb_ref, cbc_ref, dexp_ref, nw_ref,
                y_ref,
                xpx_scr, xpb_scr, xpc_scr, xcx_scr, xcb_scr, xcc_scr, st_scr):
    g = pl.program_id(1)
    first = pl.program_id(2) == 0
    gw = SSD_GROUP_WIDTH

    @pl.when(first)
    def _():
        st_scr[...] = jnp.zeros(st_scr.shape, F32)

    _conv_block(xs_ref, xpx_scr, cwx_ref, cbx_ref, xcx_scr, first, gw, True)
    _conv_block(bm_ref, xpb_scr, cwb_ref, cbb_ref, xcb_scr, first, SSD_STATE, True)
    _conv_block(cm_ref, xpc_scr, cwc_ref, cbc_ref, xcc_scr, first, SSD_STATE, True)

    causal = _iota((CHUNK, CHUNK), 1) <= _iota((CHUNK, CHUNK), 0)
    low_half = _iota((CHUNK, LANES), 1) < SSD_HEAD_DIM
    expand = ((jnp.right_shift(_iota((LANES, gw), 1), SSD_HEAD_SHIFT) + g * SSD_HPG)
              == _iota((LANES, gw), 0)).astype(F32).astype(BF16)

    for c in range(CHUNKS_PER_BLOCK):
        r0 = c * CHUNK
        xs = xcx_scr[r0:r0 + CHUNK, :]
        bm = xcb_scr[r0:r0 + CHUNK, :]
        cm16 = xcc_scr[r0:r0 + CHUNK, :].astype(BF16)
        acum_all = cum_ref[r0:r0 + CHUNK, :]
        acum_t = cum_t_ref[:, r0:r0 + CHUNK]
        dt_t = dt_t_ref[:, r0:r0 + CHUNK]

        cb = _dot_nt(cm16, bm.astype(BF16))

        ydiag = []
        for j in range(SSD_HPG // 2):
            wts = []
            for r in (2 * j, 2 * j + 1):
                acol = _column(acum_all, g * SSD_HPG + r)
                seg = acol - acum_t[r:r + 1, :]
                decay = jnp.exp(jnp.where(causal, seg, -jnp.inf))
                wts.append((cb * decay * dt_t[r:r + 1, :]).astype(BF16))
            lhs = jnp.concatenate(wts, axis=1)
            xp = xs[:, j * LANES:(j + 1) * LANES]
            rhs = jnp.concatenate([jnp.where(low_half, xp, 0.0).astype(BF16),
                                   jnp.where(low_half, 0.0, xp).astype(BF16)], axis=0)
            ydiag.append(_dot(lhs, rhs))
        y = jnp.concatenate(ydiag, axis=1)

        q1 = q1_ref[r0:r0 + CHUNK, :]
        q2 = q2_ref[r0:r0 + CHUNK, :]
        q1h = q1.astype(BF16)
        q2h = q2.astype(BF16)
        q1l = (q1 - q1h.astype(F32)).astype(BF16)
        q2l = (q2 - q2h.astype(F32)).astype(BF16)
        qe = _dot(jnp.concatenate([q1h, q1l, q2h, q2l], axis=0), expand)
        q1e = qe[0:CHUNK] + qe[CHUNK:2 * CHUNK]
        q2e = qe[2 * CHUNK:3 * CHUNK] + qe[3 * CHUNK:4 * CHUNK]

        prev = st_scr[...]
        states = _dot(bm.T.astype(BF16), (q1e * xs).astype(BF16))
        y = y + _dot(cm16, prev.astype(BF16)) * q2e
        st_scr[...] = prev * q2e[CHUNK - 1:CHUNK, :] + states

        y = y + dexp_ref[...] * xs
        y = y * _silu(z_ref[r0:r0 + CHUNK, :])
        y = y * lax.rsqrt(jnp.mean(y * y, axis=1, keepdims=True) + 1e-6)
        y_ref[r0:r0 + CHUNK, :] = (y * nw_ref[...]).astype(BF16)


def _ssd_mixer(proj, cum, q1, q2, cum_t, pd_t, batch, seq, layer, conv_w, conv_b, d_exp, norm_w):
    t = proj.shape[0]
    nb = seq // TIME_BLOCK
    tb = TIME_BLOCK
    gw = SSD_GROUP_WIDTH

    def col_spec(width, col0):
        base = col0 // width
        return pl.BlockSpec((tb, width), lambda b, g, i: (b * nb + i, base + g))

    def w_spec(nrows, width, col0):
        base = col0 // width
        return pl.BlockSpec((None, nrows, width), lambda b, g, i: (layer, 0, base + g))

    seq_blk = pl.BlockSpec((tb, LANES), lambda b, g, i: (b * nb + i, 0))
    t_blk = pl.BlockSpec((None, SSD_HPG, tb), lambda b, g, i: (b * nb + i, g, 0))
    cx, cb_, cc = 0, SSD_WIDTH, SSD_WIDTH + SSD_GROUPS * SSD_STATE
    return pl.pallas_call(
        _ssd_kernel,
        out_shape=jax.ShapeDtypeStruct((t, SSD_WIDTH), BF16),
        grid=(batch, SSD_GROUPS, nb),
        in_specs=[col_spec(gw, COL_Z), col_spec(gw, COL_XS), col_spec(SSD_STATE, COL_B),
                  col_spec(SSD_STATE, COL_C),
                  seq_blk, seq_blk, seq_blk, t_blk, t_blk,
                  w_spec(CONV_WIDTH, gw, cx), w_spec(CONV_WIDTH, SSD_STATE, cb_),
                  w_spec(CONV_WIDTH, SSD_STATE, cc),
                  w_spec(1, gw, cx), w_spec(1, SSD_STATE, cb_), w_spec(1, SSD_STATE, cc),
                  w_spec(1, gw, 0), w_spec(1, gw, 0)],
        out_specs=pl.BlockSpec((tb, gw), lambda b, g, i: (b * nb + i, g)),
        scratch_shapes=[pltpu.VMEM((tb + 2 * SUBLANES, gw), F32),
                        pltpu.VMEM((tb + 2 * SUBLANES, SSD_STATE), F32),
                        pltpu.VMEM((tb + 2 * SUBLANES, SSD_STATE), F32),
                        pltpu.VMEM((tb, gw), F32),
                        pltpu.VMEM((tb, SSD_STATE), F32),
                        pltpu.VMEM((tb, SSD_STATE), F32),
                        pltpu.VMEM((SSD_STATE, gw), F32)],
        compiler_params=_params(3),
        name="ssd_mixer",
    )(proj, proj, proj, proj, cum, q1, q2, cum_t, pd_t,
      conv_w, conv_w, conv_w, conv_b, conv_b, conv_b, d_exp, norm_w)


def _lru_kernel(x_ref, gate_ref, cw_ref, cb_ref, wa_ref, wx_ref, ba_ref, bx_ref, lam_ref,
                y_ref, xp_scr, xc_scr, h_scr):
    first = pl.program_id(2) == 0

    @pl.when(first)
    def _():
        h_scr[...] = jnp.zeros(h_scr.shape, F32)

    _conv_block(x_ref, xp_scr, cw_ref, cb_ref, xc_scr, first, LRU_BLOCK_DIM, False)
    row_in_tile = jnp.bitwise_and(_iota((CHUNK, LANES), 0), SUBLANES - 1)
    sp_lam = _softplus(-lam_ref[...])
    carry = h_scr[...]

    for c in range(CHUNKS_PER_BLOCK):
        r0 = c * CHUNK
        xc = xc_scr[r0:r0 + CHUNK, :]
        xc16 = xc.astype(BF16)
        r = _sigmoid(_dot(xc16, wa_ref[0]) + ba_ref[...])
        i = _sigmoid(_dot(xc16, wx_ref[0]) + bx_ref[...])
        log_a = -LRU_C * r * sp_lam
        a = jnp.exp(log_a)
        var = -jnp.tanh(log_a) * (a * a + 1.0)
        u = jnp.where(var > 0.0, var * lax.rsqrt(var), 0.0) * (i * xc)
        d = 1
        while d < SUBLANES:
            keep = row_in_tile >= d
            a_sh = pltpu.roll(a, d, 0)
            u_sh = pltpu.roll(u, d, 0)
            u = jnp.where(keep, u + a * u_sh, u)
            a = jnp.where(keep, a * a_sh, a)
            d *= 2
        hs = []
        for t in range(CHUNK // SUBLANES):
            s0 = t * SUBLANES
            hs.append(u[s0:s0 + SUBLANES, :] + a[s0:s0 + SUBLANES, :] * carry)
            carry = hs[-1][SUBLANES - 1:SUBLANES, :]
        h = jnp.concatenate(hs, axis=0)
        y_ref[r0:r0 + CHUNK, :] = (h * _silu(gate_ref[r0:r0 + CHUNK, :])).astype(BF16)

    h_scr[...] = carry


def _lru_mixer(proj, batch, seq, layer, conv_w, conv_b, w_a, w_x, b_a, b_x, lam):
    t = proj.shape[0]
    nb = seq // TIME_BLOCK
    tb = TIME_BLOCK
    bd = LRU_BLOCK_DIM

    def col_spec(col0):
        base = col0 // bd
        return pl.BlockSpec((tb, bd), lambda b, h, i: (b * nb + i, base + h))

    vec = pl.BlockSpec((None, 1, bd), lambda b, h, i: (layer, 0, h))
    wspec = pl.BlockSpec((None, 1, bd, bd), lambda b, h, i: (layer, h, 0, 0))
    return pl.pallas_call(
        _lru_kernel,
        out_shape=jax.ShapeDtypeStruct((t, LRU_WIDTH), BF16),
        grid=(batch, LRU_BLOCKS, nb),
        in_specs=[col_spec(COL_LRU_X), col_spec(COL_LRU_G),
                  pl.BlockSpec((None, CONV_WIDTH, bd), lambda b, h, i: (layer, 0, h)), vec,
                  wspec, wspec, vec, vec, vec],
        out_specs=pl.BlockSpec((tb, bd), lambda b, h, i: (b * nb + i, h)),
        scratch_shapes=[pltpu.VMEM((tb + 2 * SUBLANES, bd), F32),
                        pltpu.VMEM((tb, bd), F32),
                        pltpu.VMEM((1, bd), F32)],
        compiler_params=_params(3),
        name="lru_mixer",
    )(proj, proj, conv_w, conv_b, w_a, w_x, b_a, b_x, lam)


def _mlstm_kernel(x_ref, o_ref, gate_ref, cum_ref, pd_ref, cum_t_ref, pd_t_ref, cw_ref, cb_ref,
                  wq_ref, wk_ref, wv_ref, nw_ref,
                  y_ref, xp_scr, xc_scr, c_scr, n_scr, m_scr):
    h = pl.program_id(1)
    first = pl.program_id(2) == 0
    dh = MLSTM_HEAD_DIM

    @pl.when(first)
    def _():
        c_scr[...] = jnp.zeros(c_scr.shape, F32)
        n_scr[...] = jnp.zeros(n_scr.shape, F32)
        m_scr[...] = jnp.zeros(m_scr.shape, F32)

    _conv_block(x_ref, xp_scr, cw_ref, cb_ref, xc_scr, first, dh, True)
    causal = _iota((CHUNK, CHUNK), 1) <= _iota((CHUNK, CHUNK), 0)
    is_head = _iota((MLSTM_HEADS, CHUNK), 0) == h

    for c in range(CHUNKS_PER_BLOCK):
        r0 = c * CHUNK
        xm16 = x_ref[r0:r0 + CHUNK, :].astype(BF16)
        xc16 = xc_scr[r0:r0 + CHUNK, :].astype(BF16)
        q = _dot(xc16, wq_ref[0])
        k = _dot(xc16, wk_ref[0]) * (dh ** -0.5)
        v = _dot(xm16, wv_ref[0])
        q16 = q.astype(BF16)
        v16 = v.astype(BF16)

        i_row = jnp.sum(jnp.where(is_head, pd_t_ref[:, r0:r0 + CHUNK], 0.0), axis=0, keepdims=True)
        b_row = jnp.sum(jnp.where(is_head, cum_t_ref[:, r0:r0 + CHUNK], 0.0), axis=0, keepdims=True)
        i_col = _column(pd_ref[r0:r0 + CHUNK, :], SMALL_I + h)
        b_col = _column(cum_ref[r0:r0 + CHUNK, :], SMALL_F + h)

        m_prev = m_scr[...]
        a_inter = b_col + m_prev
        d_intra = jnp.where(causal, b_col - b_row + i_row, -jnp.inf)
        m_t = jnp.maximum(a_inter, jnp.max(d_intra, axis=1, keepdims=True))
        w_intra = jnp.exp(d_intra - m_t)
        w_inter = jnp.exp(a_inter - m_t)
        s = _dot_nt(q16, k.astype(BF16)) * w_intra
        c_st = c_scr[...]
        n_st = n_scr[...]
        num = _dot(s.astype(BF16), v16) + w_inter * _dot(q16, c_st.astype(BF16))
        den = jnp.sum(s, axis=1, keepdims=True) + w_inter * jnp.sum(q * n_st, axis=1, keepdims=True)
        hs = num / jnp.maximum(jnp.abs(den), jnp.exp(-m_t))

        m_new = m_t[CHUNK - 1:CHUNK, :]
        b_last = b_col[CHUNK - 1:CHUNK, :]
        w_s = jnp.exp(b_last - b_col + i_col - m_new)
        carry_decay = jnp.exp(b_last + m_prev - m_new)
        kw = k * w_s
        c_scr[...] = carry_decay * c_st + _dot(kw.T.astype(BF16), v16)
        n_scr[...] = carry_decay * n_st + jnp.sum(kw, axis=0, keepdims=True)
        m_scr[...] = m_new

        hn = hs * lax.rsqrt(jnp.mean(hs * hs, axis=1, keepdims=True) + 1e-6) * nw_ref[...]
        out = hn * _sigmoid(o_ref[r0:r0 + CHUNK, :]) * _silu(gate_ref[r0:r0 + CHUNK, :])
        y_ref[r0:r0 + CHUNK, :] = out.astype(BF16)


def _mlstm_mixer(proj, cum, pd, cum_t, pd_t, batch, seq, layer, conv_w, conv_b, w_q, w_k, w_v, norm_w):
    t = proj.shape[0]
    nb = seq // TIME_BLOCK
    tb = TIME_BLOCK
    dh = MLSTM_HEAD_DIM

    def col_spec(col0):
        base = col0 // dh
        return pl.BlockSpec((tb, dh), lambda b, h, i: (b * nb + i, base + h))

    def t_blk(row0):
        return pl.BlockSpec((None, MLSTM_HEADS, tb), lambda b, h, i: (b * nb + i, row0 // MLSTM_HEADS, 0))

    seq_blk = pl.BlockSpec((tb, LANES), lambda b, h, i: (b * nb + i, 0))
    vec = pl.BlockSpec((None, 1, dh), lambda b, h, i: (layer, 0, h))
    wspec = pl.BlockSpec((None, 1, dh, dh), lambda b, h, i: (layer, h, 0, 0))
    return pl.pallas_call(
        _mlstm_kernel,
        out_shape=jax.ShapeDtypeStruct((t, MLSTM_WIDTH), BF16),
        grid=(batch, MLSTM_HEADS, nb),
        in_specs=[col_spec(COL_ML_X), col_spec(COL_ML_O), col_spec(COL_ML_G),
                  seq_blk, seq_blk, t_blk(SMALL_F), t_blk(SMALL_I),
                  pl.BlockSpec((None, CONV_WIDTH, dh), lambda b, h, i: (layer, 0, h)), vec,
                  wspec, wspec, wspec, vec],
        out_specs=pl.BlockSpec((tb, dh), lambda b, h, i: (b * nb + i, h)),
        scratch_shapes=[pltpu.VMEM((tb + 2 * SUBLANES, dh), F32),
                        pltpu.VMEM((tb, dh), F32),
                        pltpu.VMEM((dh, dh), F32),
                        pltpu.VMEM((1, dh), F32),
                        pltpu.VMEM((1, 1), F32)],
        compiler_params=_params(3),
        name="mlstm_mixer",
    )(proj, proj, proj, cum, pd, cum_t, pd_t, conv_w, conv_b, w_q, w_k, w_v, norm_w)


OUT_TM = 512
OUT_TK = 512
OUT_K_SSD = SSD_WIDTH // OUT_TK
OUT_K_LRU = LRU_WIDTH // OUT_TK
OUT_K_ML = MLSTM_WIDTH // OUT_TK
OUT_K_STEPS = OUT_K_SSD + OUT_K_LRU + OUT_K_ML
OUT_X_ROWS = OUT_TM // OUT_K_STEPS
LN_ROWS = 32


def _out_ln_kernel(ys_ref, yl_ref, ym_ref, w_ref, x_ref, lnw_ref, lnb_ref, of_ref, ob_ref):
    k = pl.program_id(1)

    @pl.when(k == 0)
    def _():
        of_ref[...] = _dot(ys_ref[...], w_ref[...])

    @pl.when(jnp.logical_and(k > 0, k < OUT_K_SSD))
    def _():
        of_ref[...] += _dot(ys_ref[...], w_ref[...])

    @pl.when(jnp.logical_and(k >= OUT_K_SSD, k < OUT_K_SSD + OUT_K_LRU))
    def _():
        of_ref[...] += _dot(yl_ref[...], w_ref[...])

    @pl.when(k >= OUT_K_SSD + OUT_K_LRU)
    def _():
        of_ref[...] += _dot(ym_ref[...], w_ref[...])

    xr = pl.ds(pl.multiple_of(k * OUT_X_ROWS, OUT_X_ROWS), OUT_X_ROWS)
    of_ref[xr, :] += DEEPNORM_ALPHA * x_ref[...]

    @pl.when(k == OUT_K_STEPS - 1)
    def _():
        def ln(r, carry):
            r0 = pl.multiple_of(r * LN_ROWS, LN_ROWS)
            v = of_ref[pl.ds(r0, LN_ROWS), :]
            mu = jnp.mean(v, axis=1, keepdims=True)
            vc = v - mu
            var = jnp.mean(vc * vc, axis=1, keepdims=True)
            o = vc * lax.rsqrt(var + 1e-5) * lnw_ref[...] + lnb_ref[...]
            of_ref[pl.ds(r0, LN_ROWS), :] = o
            ob_ref[pl.ds(r0, LN_ROWS), :] = o.astype(BF16)
            return carry

        lax.fori_loop(0, OUT_TM // LN_ROWS, ln, 0, unroll=2)


def _out_ln(y_ssd, y_lru, y_ml, w_out16, x, layer, ln_w, ln_b):
    t = x.shape[0]
    tm, tk = OUT_TM, OUT_TK
    k1, k2 = OUT_K_SSD, OUT_K_SSD + OUT_K_LRU
    vec = pl.BlockSpec((None, 1, D_MODEL), lambda i, k: (layer, 0, 0))
    return pl.pallas_call(
        _out_ln_kernel,
        out_shape=(jax.ShapeDtypeStruct((t, D_MODEL), F32), jax.ShapeDtypeStruct((t, D_MODEL), BF16)),
        grid=(t // tm, OUT_K_STEPS),
        in_specs=[pl.BlockSpec((tm, tk), lambda i, k: (i, jnp.minimum(k, k1 - 1))),
                  pl.BlockSpec((tm, tk), lambda i, k: (i, jnp.clip(k - k1, 0, OUT_K_LRU - 1))),
                  pl.BlockSpec((tm, tk), lambda i, k: (i, jnp.clip(k - k2, 0, OUT_K_ML - 1))),
                  pl.BlockSpec((None, tk, D_MODEL), lambda i, k: (layer, k, 0)),
                  pl.BlockSpec((OUT_X_ROWS, D_MODEL), lambda i, k: (i * OUT_K_STEPS + k, 0)),
                  vec, vec],
        out_specs=(pl.BlockSpec((tm, D_MODEL), lambda i, k: (i, 0)),
                   pl.BlockSpec((tm, D_MODEL), lambda i, k: (i, 0))),
        compiler_params=_params(2),
        name="out_proj_ln",
    )(y_ssd, y_lru, y_ml, w_out16, x, ln_w, ln_b)


def _row3(p):
    return p.reshape(p.shape[0], 1, p.shape[1])


def kernel(x, w_in, ssd_conv_w, ssd_conv_b, ssd_dt_bias, ssd_a_log, ssd_d, ssd_norm_w, lru_conv_w, lru_conv_b,
           lru_w_a, lru_b_a, lru_w_x, lru_b_x, lru_lambda, mlstm_conv_w, mlstm_conv_b, mlstm_w_q, mlstm_w_k,
           mlstm_w_v, mlstm_b_i, mlstm_b_f, mlstm_norm_w, w_out, ln_w, ln_b):
    batch, seq, d = x.shape
    depth = w_in.shape[0]
    assert d == D_MODEL and seq % TIME_BLOCK == 0 and depth == DEPTH and w_in.shape[2] == N_IN
    xf = x.reshape(batch * seq, d)
    x16 = _cast_bf16(xf)

    wt = jnp.swapaxes(w_in, 1, 2)
    n_pad = LANES - SSD_HEADS - 2 * MLSTM_HEADS
    wt_small = jnp.concatenate([wt[:, N_HEAD:N_HEAD + SSD_HEADS], wt[:, N_IN - 2 * MLSTM_HEADS:],
                                jnp.zeros((depth, n_pad, d), F32)], axis=1)
    w_out16 = _cast_bf16(w_out.reshape(depth * MIX_WIDTH, d)).reshape(depth, MIX_WIDTH, d)
    small_bias = _row3(jnp.concatenate([ssd_dt_bias, mlstm_b_i, mlstm_b_f, jnp.zeros((depth, n_pad), F32)], axis=1))
    alog_row = _row3(jnp.concatenate([ssd_a_log, jnp.zeros((depth, LANES - SSD_HEADS), F32)], axis=1))
    d_exp = _row3(jnp.repeat(ssd_d, SSD_HEAD_DIM, axis=1))
    lru_wa16, lru_wx16 = lru_w_a.astype(BF16), lru_w_x.astype(BF16)
    wq16, wk16, wv16 = mlstm_w_q.astype(BF16), mlstm_w_k.astype(BF16), mlstm_w_v.astype(BF16)

    for l in range(depth):
        proj = _in_proj(x16, wt, l)
        small = _in_proj_small(x16, wt_small, l)
        cum, pd, q1, q2, cum_t, pd_t = _head_prep(small, small_bias, alog_row, l)
        y_ssd = _ssd_mixer(proj, cum, q1, q2, cum_t, pd_t, batch, seq, l, ssd_conv_w, _row3(ssd_conv_b),
                           d_exp, _row3(ssd_norm_w))
        y_lru = _lru_mixer(proj, batch, seq, l, lru_conv_w, _row3(lru_conv_b), lru_wa16, lru_wx16,
                           _row3(lru_b_a), _row3(lru_b_x), _row3(lru_lambda))
        y_ml = _mlstm_mixer(proj, cum, pd, cum_t, pd_t, batch, seq, l, mlstm_conv_w, _row3(mlstm_conv_b),
                            wq16, wk16, wv16, _row3(mlstm_norm_w))
        xf, x16 = _out_ln(y_ssd, y_lru, y_ml, w_out16, xf, l, _row3(ln_w), _row3(ln_b))
    return xf.reshape(batch, seq, d)
```

```python
import jax
import jax.numpy as jnp
from jax import lax
from jax.experimental import pallas as pl
from jax.experimental.pallas import tpu as pltpu

F32 = jnp.float32
BF16 = jnp.bfloat16

D_MODEL = 4096
DEPTH = 2
MIX_WIDTH = 2 * D_MODEL
SSD_WIDTH = MIX_WIDTH // 2
LRU_WIDTH = MIX_WIDTH // 4
MLSTM_WIDTH = MIX_WIDTH - SSD_WIDTH - LRU_WIDTH
SSD_HEAD_DIM = 64
SSD_HEAD_SHIFT = 6
SSD_HEADS = SSD_WIDTH // SSD_HEAD_DIM
SSD_GROUPS = 8
SSD_HPG = SSD_HEADS // SSD_GROUPS
SSD_STATE = 128
SSD_GROUP_WIDTH = SSD_WIDTH // SSD_GROUPS
LRU_BLOCKS = 16
LRU_BLOCK_DIM = LRU_WIDTH // LRU_BLOCKS
LRU_C = 8.0
MLSTM_HEADS = 8
MLSTM_HEAD_DIM = MLSTM_WIDTH // MLSTM_HEADS
CONV_WIDTH = 4
CHUNK = 128
DEEPNORM_ALPHA = (2.0 * DEPTH) ** 0.25

SUBLANES = 8
LANES = 128

XBC_WIDTH = SSD_WIDTH + 2 * SSD_GROUPS * SSD_STATE
N_HEAD = SSD_WIDTH + XBC_WIDTH
N_TAIL = 2 * LRU_WIDTH + 3 * MLSTM_WIDTH
N_MAIN = N_HEAD + N_TAIL
N_IN = N_HEAD + SSD_HEADS + N_TAIL + 2 * MLSTM_HEADS
COL_Z = 0
COL_XS = SSD_WIDTH
COL_B = COL_XS + SSD_WIDTH
COL_C = COL_B + SSD_GROUPS * SSD_STATE
COL_LRU_X = COL_C + SSD_GROUPS * SSD_STATE
COL_LRU_G = COL_LRU_X + LRU_WIDTH
COL_ML_X = COL_LRU_G + LRU_WIDTH
COL_ML_O = COL_ML_X + MLSTM_WIDTH
COL_ML_G = COL_ML_O + MLSTM_WIDTH
SMALL_I = SSD_HEADS
SMALL_F = SSD_HEADS + MLSTM_HEADS

VMEM_LIMIT = 56 * 1024 * 1024
TIME_BLOCK = 1024
CHUNKS_PER_BLOCK = TIME_BLOCK // CHUNK


def _params(n_axes):
    return pltpu.CompilerParams(dimension_semantics=("arbitrary",) * n_axes, vmem_limit_bytes=VMEM_LIMIT)


def _sigmoid(x):
    return 0.5 * jnp.tanh(0.5 * x) + 0.5


def _silu(x):
    h = 0.5 * x
    return h + h * jnp.tanh(h)


def _softplus(x):
    return jnp.maximum(x, 0.0) + jnp.log1p(jnp.exp(-jnp.abs(x)))


def _dot(a, b):
    return jnp.dot(a, b, preferred_element_type=F32)


def _dot_nt(a, b):
    return lax.dot_general(a, b, (((1,), (1,)), ((), ())), preferred_element_type=F32)


def _dot_f32(a, b):
    return jnp.dot(a, b, precision=lax.Precision.HIGHEST, preferred_element_type=F32)


def _iota(shape, dim):
    return lax.broadcasted_iota(jnp.int32, shape, dim)


def _column(x, idx):
    lane = _iota(x.shape, 1)
    return jnp.sum(jnp.where(lane == idx, x, 0.0), axis=1, keepdims=True)


def _conv_block(x_ref, xp_scr, w_ref, b_ref, out_scr, first, width, apply_silu):
    tb = TIME_BLOCK

    @pl.when(first)
    def _():
        xp_scr[0:SUBLANES, :] = jnp.zeros((SUBLANES, width), F32)

    @pl.when(jnp.logical_not(first))
    def _():
        xp_scr[0:SUBLANES, :] = xp_scr[tb:tb + SUBLANES, :]

    xp_scr[SUBLANES:SUBLANES + tb, :] = x_ref[...]
    for c in range(CHUNKS_PER_BLOCK):
        r0 = c * CHUNK
        for l0 in range(0, width, LANES):
            acc = b_ref[:, l0:l0 + LANES] + w_ref[3:4, l0:l0 + LANES] * xp_scr[r0 + 8:r0 + 8 + CHUNK, l0:l0 + LANES]
            for k in range(CONV_WIDTH - 1):
                off = r0 + 8 - (CONV_WIDTH - 1) + k
                acc = acc + w_ref[k:k + 1, l0:l0 + LANES] * xp_scr[off:off + CHUNK, l0:l0 + LANES]
            if apply_silu:
                acc = _silu(acc)
            out_scr[r0:r0 + CHUNK, l0:l0 + LANES] = acc


CAST_ROWS = 512


def _cast_kernel(a_ref, o_ref):
    o_ref[...] = a_ref[...].astype(BF16)


def _cast_bf16(a):
    r, c = a.shape
    return pl.pallas_call(
        _cast_kernel,
        out_shape=jax.ShapeDtypeStruct((r, c), BF16),
        grid=(r // CAST_ROWS,),
        in_specs=[pl.BlockSpec((CAST_ROWS, c), lambda i: (i, 0))],
        out_specs=pl.BlockSpec((CAST_ROWS, c), lambda i: (i, 0)),
        compiler_params=_params(1),
        name="cast_bf16",
    )(a)


IN_TM = 2048
IN_TN = 512
IN_SMALL_TM = 1024


def _in_proj_kernel(a_ref, wt_ref, o_ref):
    o_ref[...] = _dot_nt(a_ref[...], wt_ref[...].astype(BF16))


def _in_proj(x16, wt_all, layer):
    m, k = x16.shape
    head_blocks = N_HEAD // IN_TN

    def w_index(i, j):
        row = j * IN_TN + jnp.where(j >= head_blocks, SSD_HEADS, 0)
        return (layer, pl.multiple_of(row, SSD_HEADS), 0)

    return pl.pallas_call(
        _in_proj_kernel,
        out_shape=jax.ShapeDtypeStruct((m, N_MAIN), F32),
        grid=(m // IN_TM, N_MAIN // IN_TN),
        in_specs=[pl.BlockSpec((IN_TM, k), lambda i, j: (i, 0), pipeline_mode=pl.Buffered(1)),
                  pl.BlockSpec((None, pl.Element(IN_TN), pl.Element(k)), w_index)],
        out_specs=pl.BlockSpec((IN_TM, IN_TN), lambda i, j: (i, j)),
        compiler_params=_params(2),
        name="in_proj",
    )(x16, wt_all)


def _in_proj_small(x16, wt_small, layer):
    m, k = x16.shape
    return pl.pallas_call(
        _in_proj_kernel,
        out_shape=jax.ShapeDtypeStruct((m, LANES), F32),
        grid=(m // IN_SMALL_TM,),
        in_specs=[pl.BlockSpec((IN_SMALL_TM, k), lambda i: (i, 0)),
                  pl.BlockSpec((None, LANES, k), lambda i: (layer, 0, 0))],
        out_specs=pl.BlockSpec((IN_SMALL_TM, LANES), lambda i: (i, 0)),
        compiler_params=_params(1),
        name="in_proj_small",
    )(x16, wt_small)


def _head_prep_kernel(sm_ref, bias_ref, alog_ref, cum_ref, pd_ref, cd_ref, cum_t_ref, pd_t_ref, q1_t_ref):
    ltri = (_iota((CHUNK, CHUNK), 1) <= _iota((CHUNK, CHUNK), 0)).astype(F32)
    lane = _iota((1, LANES), 1)
    is_ssd = lane < SSD_HEADS
    a_neg = jnp.where(is_ssd, -jnp.exp(alog_ref[...]), 0.0)
    chunk_decay = []
    for c in range(CHUNKS_PER_BLOCK):
        r0 = c * CHUNK
        pre = sm_ref[r0:r0 + CHUNK, :] + bias_ref[...]
        dt = _softplus(pre)
        log_f = -_softplus(-pre)
        cum = _dot_f32(ltri, jnp.where(is_ssd, dt * a_neg, log_f))
        pd = jnp.where(is_ssd, dt, pre)
        cum_ref[r0:r0 + CHUNK, :] = cum
        pd_ref[r0:r0 + CHUNK, :] = pd
        cum_last = cum[CHUNK - 1:CHUNK, :]
        cum_t_ref[:, r0:r0 + CHUNK] = cum.T
        pd_t_ref[:, r0:r0 + CHUNK] = pd.T
        q1_t_ref[:, r0:r0 + CHUNK] = (jnp.exp(cum_last - cum) * dt).T
        chunk_decay.append(jnp.exp(cum_last))
    cd = jnp.concatenate(chunk_decay, axis=0)
    cd_hi = cd.astype(BF16)
    cd_lo = (cd - cd_hi.astype(F32)).astype(BF16)
    expand = (jnp.right_shift(_iota((LANES, SSD_WIDTH), 1), SSD_HEAD_SHIFT)
              == _iota((LANES, SSD_WIDTH), 0)).astype(F32).astype(BF16)
    cd_ref[...] = _dot(cd_hi, expand) + _dot(cd_lo, expand)


def _head_prep(small, bias_all, alog_all, layer):
    t = small.shape[0]
    tb = TIME_BLOCK
    nb = t // tb
    row_blk = pl.BlockSpec((tb, LANES), lambda i: (i, 0))
    vec = pl.BlockSpec((None, 1, LANES), lambda i: (layer, 0, 0))
    t_blk = pl.BlockSpec((None, LANES, tb), lambda i: (i, 0, 0))
    flat = jax.ShapeDtypeStruct((t, LANES), F32)
    transposed = jax.ShapeDtypeStruct((nb, LANES, tb), F32)
    cd_shape = jax.ShapeDtypeStruct((t // CHUNK, SSD_WIDTH), F32)
    cd_blk = pl.BlockSpec((CHUNKS_PER_BLOCK, SSD_WIDTH), lambda i: (i, 0))
    return pl.pallas_call(
        _head_prep_kernel,
        out_shape=(flat, flat, cd_shape, transposed, transposed, transposed),
        grid=(nb,),
        in_specs=[row_blk, vec, vec],
        out_specs=(row_blk, row_blk, cd_blk, t_blk, t_blk, t_blk),
        compiler_params=_params(1),
        name="head_prep",
    )(small, bias_all, alog_all)


def _ssd_kernel(z_ref, xs_ref, bm_ref, cm_ref, cum_ref, cd_ref, cum_t_ref, dt_t_ref, q1_t_ref,
                cwx_ref, cwb_ref, cwc_ref, cbx_ref, cbb_ref, cbc_ref, dexp_ref, nw_ref,
                y_ref,
                xpx_scr, xpb_scr, xpc_scr, xcx_scr, xcb_scr, xcc_scr, st_scr):
    g = pl.program_id(1)
    first = pl.program_id(2) == 0
    gw = SSD_GROUP_WIDTH

    @pl.when(first)
    def _():
        st_scr[...] = jnp.zeros(st_scr.shape, F32)

    _conv_block(xs_ref, xpx_scr, cwx_ref, cbx_ref, xcx_scr, first, gw, True)
    _conv_block(bm_ref, xpb_scr, cwb_ref, cbb_ref, xcb_scr, first, SSD_STATE, True)
    _conv_block(cm_ref, xpc_scr, cwc_ref, cbc_ref, xcc_scr, first, SSD_STATE, True)

    causal = _iota((CHUNK, CHUNK), 1) <= _iota((CHUNK, CHUNK), 0)
    low_half = _iota((CHUNK, LANES), 1) < SSD_HEAD_DIM
    prev = st_scr[...]

    for c in range(CHUNKS_PER_BLOCK):
        r0 = c * CHUNK
        xs = xcx_scr[r0:r0 + CHUNK, :]
        bm = xcb_scr[r0:r0 + CHUNK, :]
        cm16 = xcc_scr[r0:r0 + CHUNK, :].astype(BF16)
        acum_all = cum_ref[r0:r0 + CHUNK, :]
        acum_t = cum_t_ref[:, r0:r0 + CHUNK]
        dt_t = dt_t_ref[:, r0:r0 + CHUNK]
        q1_t = q1_t_ref[:, r0:r0 + CHUNK]

        cb = _dot_nt(cm16, bm.astype(BF16))
        bm_t = bm.T
        c_prev =_dot(cm16, prev.astype(BF16))

        ys, sts = [], []
        for j in range(SSD_HPG // 2):
            pair = slice(j * LANES, (j + 1) * LANES)
            wts, bts, q2 = [], [], []
            for r in (2 * j, 2 * j + 1):
                acol = _column(acum_all, g * SSD_HPG + r)
                seg = acol - acum_t[r:r + 1, :]
                decay = jnp.exp(jnp.where(causal, seg, -jnp.inf))
                wts.append((cb * decay * dt_t[r:r + 1, :]).astype(BF16))
                bts.append((bm_t * q1_t[r:r + 1, :]).astype(BF16))
                q2.append(jnp.exp(acol))
            xp = xs[:, pair]
            rhs = jnp.concatenate([jnp.where(low_half, xp, 0.0).astype(BF16),
                                   jnp.where(low_half, 0.0, xp).astype(BF16)], axis=0)
            y_diag = _dot(jnp.concatenate(wts, axis=1), rhs)
            sts.append(_dot(jnp.concatenate(bts, axis=1), rhs))
            ys.append(y_diag + c_prev[:, pair] * jnp.where(low_half, q2[0], q2[1]))
        y = jnp.concatenate(ys, axis=1)
        prev = prev * cd_ref[c:c + 1, :] + jnp.concatenate(sts, axis=1)

        y = y + dexp_ref[...] * xs
        y = y * _silu(z_ref[r0:r0 + CHUNK, :])
        y = y * lax.rsqrt(jnp.mean(y * y, axis=1, keepdims=True) + 1e-6)
        y_ref[r0:r0 + CHUNK, :] = (y * nw_ref[...]).astype(BF16)

    st_scr[...] = prev


def _ssd_mixer(proj, cum, cd, cum_t, pd_t, q1_t, batch, seq, layer, conv_w, conv_b, d_exp, norm_w):
    t = proj.shape[0]
    nb = seq // TIME_BLOCK
    tb = TIME_BLOCK
    gw = SSD_GROUP_WIDTH

    def col_spec(width, col0):
        base = col0 // width
        return pl.BlockSpec((tb, width), lambda b, g, i: (b * nb + i, base + g))

    def w_spec(nrows, width, col0):
        base = col0 // width
        return pl.BlockSpec((None, nrows, width), lambda b, g, i: (layer, 0, base + g))

    seq_blk = pl.BlockSpec((tb, LANES), lambda b, g, i: (b * nb + i, 0))
    t_blk = pl.BlockSpec((None, SSD_HPG, tb), lambda b, g, i: (b * nb + i, g, 0))
    cd_blk = pl.BlockSpec((CHUNKS_PER_BLOCK, gw), lambda b, g, i: (b * nb + i, g))
    cx, cb_, cc = 0, SSD_WIDTH, SSD_WIDTH + SSD_GROUPS * SSD_STATE
    return pl.pallas_call(
        _ssd_kernel,
        out_shape=jax.ShapeDtypeStruct((t, SSD_WIDTH), BF16),
        grid=(batch, SSD_GROUPS, nb),
        in_specs=[col_spec(gw, COL_Z), col_spec(gw, COL_XS), col_spec(SSD_STATE, COL_B),
                  col_spec(SSD_STATE, COL_C),
                  seq_blk, cd_blk, t_blk, t_blk, t_blk,
                  w_spec(CONV_WIDTH, gw, cx), w_spec(CONV_WIDTH, SSD_STATE, cb_),
                  w_spec(CONV_WIDTH, SSD_STATE, cc),
                  w_spec(1, gw, cx), w_spec(1, SSD_STATE, cb_), w_spec(1, SSD_STATE, cc),
                  w_spec(1, gw, 0), w_spec(1, gw, 0)],
        out_specs=pl.BlockSpec((tb, gw), lambda b, g, i: (b * nb + i, g)),
        scratch_shapes=[pltpu.VMEM((tb + 2 * SUBLANES, gw), F32),
                        pltpu.VMEM((tb + 2 * SUBLANES, SSD_STATE), F32),
                        pltpu.VMEM((tb + 2 * SUBLANES, SSD_STATE), F32),
                        pltpu.VMEM((tb, gw), F32),
                        pltpu.VMEM((tb, SSD_STATE), F32),
                        pltpu.VMEM((tb, SSD_STATE), F32),
                        pltpu.VMEM((SSD_STATE, gw), F32)],
        compiler_params=_params(3),
        name="ssd_mixer",
    )(proj, proj, proj, proj, cum, cd, cum_t, pd_t, q1_t,
      conv_w, conv_w, conv_w, conv_b, conv_b, conv_b, d_exp, norm_w)


def _lru_kernel(x_ref, gate_ref, cw_ref, cb_ref, wa_ref, wx_ref, ba_ref, bx_ref, lam_ref,
                y_ref, xp_scr, xc_scr, h_scr):
    first = pl.program_id(2) == 0

    @pl.when(first)
    def _():
        h_scr[...] = jnp.zeros(h_scr.shape, F32)

    _conv_block(x_ref, xp_scr, cw_ref, cb_ref, xc_scr, first, LRU_BLOCK_DIM, False)
    row_in_tile = _iota((CHUNK // SUBLANES, SUBLANES, LANES), 1)
    keep = {d: row_in_tile >= d for d in (1, 2, 4)}
    sp_lam = _softplus(-lam_ref[...])
    carry = h_scr[...]

    for c in range(CHUNKS_PER_BLOCK):
        r0 = c * CHUNK
        xc = xc_scr[r0:r0 + CHUNK, :]
        xc16 = xc.astype(BF16)
        r = _sigmoid(_dot(xc16, wa_ref[0]) + ba_ref[...])
        i = _sigmoid(_dot(xc16, wx_ref[0]) + bx_ref[...])
        log_a = -LRU_C * r * sp_lam
        a = jnp.exp(log_a)
        var = -jnp.tanh(log_a) * (a * a + 1.0)
        u = jnp.where(var > 0.0, var * lax.rsqrt(var), 0.0) * (i * xc)
        tiles = CHUNK // SUBLANES
        a = a.reshape(tiles, SUBLANES, LANES)
        u = u.reshape(tiles, SUBLANES, LANES)
        d = 1
        while d < SUBLANES:
            a_sh = pltpu.roll(a, d, 1)
            u_sh = pltpu.roll(u, d, 1)
            u = jnp.where(keep[d], u + a * u_sh, u)
            a = jnp.where(keep[d], a * a_sh, a)
            d *= 2
        hs = []
        for t in range(tiles):
            hs.append(u[t] + a[t] * carry)
            carry = hs[-1][SUBLANES - 1:SUBLANES, :]
        h = jnp.concatenate(hs, axis=0)
        y_ref[r0:r0 + CHUNK, :] = (h * _silu(gate_ref[r0:r0 + CHUNK, :])).astype(BF16)

    h_scr[...] = carry


def _lru_mixer(proj, batch, seq, layer, conv_w, conv_b, w_a, w_x, b_a, b_x, lam):
    t = proj.shape[0]
    nb = seq // TIME_BLOCK
    tb = TIME_BLOCK
    bd = LRU_BLOCK_DIM

    def col_spec(col0):
        base = col0 // bd
        return pl.BlockSpec((tb, bd), lambda b, h, i: (b * nb + i, base + h))

    vec = pl.BlockSpec((None, 1, bd), lambda b, h, i: (layer, 0, h))
    wspec = pl.BlockSpec((None, 1, bd, bd), lambda b, h, i: (layer, h, 0, 0))
    return pl.pallas_call(
        _lru_kernel,
        out_shape=jax.ShapeDtypeStruct((t, LRU_WIDTH), BF16),
        grid=(batch, LRU_BLOCKS, nb),
        in_specs=[col_spec(COL_LRU_X), col_spec(COL_LRU_G),
                  pl.BlockSpec((None, CONV_WIDTH, bd), lambda b, h, i: (layer, 0, h)), vec,
                  wspec, wspec, vec, vec, vec],
        out_specs=pl.BlockSpec((tb, bd), lambda b, h, i: (b * nb + i, h)),
        scratch_shapes=[pltpu.VMEM((tb + 2 * SUBLANES, bd), F32),
                        pltpu.VMEM((tb, bd), F32),
                        pltpu.VMEM((1, bd), F32)],
        compiler_params=_params(3),
        name="lru_mixer",
    )(proj, proj, conv_w, conv_b, w_a, w_x, b_a, b_x, lam)


def _mlstm_kernel(x_ref, o_ref, gate_ref, cum_ref, pd_ref, cum_t_ref, pd_t_ref, cw_ref, cb_ref,
                  wq_ref, wk_ref, wv_ref, nw_ref,
                  y_ref, xp_scr, xc_scr, c_scr, n_scr, m_scr):
    h = pl.program_id(1)
    first = pl.program_id(2) == 0
    dh = MLSTM_HEAD_DIM

    @pl.when(first)
    def _():
        c_scr[...] = jnp.zeros(c_scr.shape, F32)
        n_scr[...] = jnp.zeros(n_scr.shape, F32)
        m_scr[...] = jnp.zeros(m_scr.shape, F32)

    _conv_block(x_ref, xp_scr, cw_ref, cb_ref, xc_scr, first, dh, True)
    causal = _iota((CHUNK, CHUNK), 1) <= _iota((CHUNK, CHUNK), 0)
    is_head = _iota((MLSTM_HEADS, CHUNK), 0) == h
    c_st = c_scr[...]
    n_st = n_scr[...]
    m_prev = m_scr[...]

    for c in range(CHUNKS_PER_BLOCK):
        r0 = c * CHUNK
        xm16 = x_ref[r0:r0 + CHUNK, :].astype(BF16)
        xc16 = xc_scr[r0:r0 + CHUNK, :].astype(BF16)
        q = _dot(xc16, wq_ref[0])
        k = _dot(xc16, wk_ref[0]) * (dh ** -0.5)
        v = _dot(xm16, wv_ref[0])
        q16 = q.astype(BF16)
        v16 = v.astype(BF16)

        i_row = jnp.sum(jnp.where(is_head, pd_t_ref[:, r0:r0 + CHUNK], 0.0), axis=0, keepdims=True)
        b_row = jnp.sum(jnp.where(is_head, cum_t_ref[:, r0:r0 + CHUNK], 0.0), axis=0, keepdims=True)
        i_col = _column(pd_ref[r0:r0 + CHUNK, :], SMALL_I + h)
        b_col = _column(cum_ref[r0:r0 + CHUNK, :], SMALL_F + h)

        a_inter = b_col + m_prev
        d_intra = jnp.where(causal, b_col - b_row + i_row, -jnp.inf)
        m_t = jnp.maximum(a_inter, jnp.max(d_intra, axis=1, keepdims=True))
        w_intra = jnp.exp(d_intra - m_t)
        w_inter = jnp.exp(a_inter - m_t)
        s = _dot_nt(q16, k.astype(BF16)) * w_intra
        num =_dot(s.astype(BF16), v16) + w_inter * _dot(q16, c_st.astype(BF16))
        den = jnp.sum(s, axis=1, keepdims=True) + w_inter * jnp.sum(q * n_st, axis=1, keepdims=True)
        hs = num * (1.0 / jnp.maximum(jnp.abs(den), jnp.exp(-m_t)))

        m_new = m_t[CHUNK - 1:CHUNK, :]
        b_last = b_col[CHUNK - 1:CHUNK, :]
        w_s = jnp.exp(b_last - b_col + i_col - m_new)
        carry_decay = jnp.exp(b_last + m_prev - m_new)
        kw = k * w_s
        c_st = carry_decay * c_st + _dot(kw.T.astype(BF16), v16)
        n_st = carry_decay * n_st + jnp.sum(kw, axis=0, keepdims=True)
        m_prev = m_new

        hn = hs * lax.rsqrt(jnp.mean(hs * hs, axis=1, keepdims=True) + 1e-6) * nw_ref[...]
        out = hn * _sigmoid(o_ref[r0:r0 + CHUNK, :]) * _silu(gate_ref[r0:r0 + CHUNK, :])
        y_ref[r0:r0 + CHUNK, :] = out.astype(BF16)

    c_scr[...] = c_st
    n_scr[...] = n_st
    m_scr[...] = m_prev


def _mlstm_mixer(proj, cum, pd, cum_t, pd_t, batch, seq, layer, conv_w, conv_b, w_q, w_k, w_v, norm_w):
    t = proj.shape[0]
    nb = seq // TIME_BLOCK
    tb = TIME_BLOCK
    dh = MLSTM_HEAD_DIM

    def col_spec(col0):
        base = col0 // dh
        return pl.BlockSpec((tb, dh), lambda b, h, i: (b * nb + i, base + h))

    def t_blk(row0):
        return pl.BlockSpec((None, MLSTM_HEADS, tb), lambda b, h, i: (b * nb + i, row0 // MLSTM_HEADS, 0))

    seq_blk = pl.BlockSpec((tb, LANES), lambda b, h, i: (b * nb + i, 0))
    vec = pl.BlockSpec((None, 1, dh), lambda b, h, i: (layer, 0, h))
    wspec = pl.BlockSpec((None, 1, dh, dh), lambda b, h, i: (layer, h, 0, 0))
    return pl.pallas_call(
        _mlstm_kernel,
        out_shape=jax.ShapeDtypeStruct((t, MLSTM_WIDTH), BF16),
        grid=(batch, MLSTM_HEADS, nb),
        in_specs=[col_spec(COL_ML_X), col_spec(COL_ML_O), col_spec(COL_ML_G),
                  seq_blk, seq_blk, t_blk(SMALL_F), t_blk(SMALL_I),
                  pl.BlockSpec((None, CONV_WIDTH, dh), lambda b, h, i: (layer, 0, h)), vec,
                  wspec, wspec, wspec, vec],
        out_specs=pl.BlockSpec((tb, dh), lambda b, h, i: (b * nb + i, h)),
        scratch_shapes=[pltpu.VMEM((tb + 2 * SUBLANES, dh), F32),
                        pltpu.VMEM((tb, dh), F32),
                        pltpu.VMEM((dh, dh), F32),
                        pltpu.VMEM((1, dh), F32),
                        pltpu.VMEM((1, 1), F32)],
        compiler_params=_params(3),
        name="mlstm_mixer",
    )(proj, proj, proj, cum, pd, cum_t, pd_t, conv_w, conv_b, w_q, w_k, w_v, norm_w)


OUT_TM = 512
OUT_TK = 1024
OUT_K_SSD = SSD_WIDTH // OUT_TK
OUT_K_LRU = LRU_WIDTH // OUT_TK
OUT_K_ML = MLSTM_WIDTH // OUT_TK
OUT_K_STEPS = OUT_K_SSD + OUT_K_LRU + OUT_K_ML
OUT_X_ROWS = OUT_TM // OUT_K_STEPS
LN_ROWS = 32


def _out_ln_kernel(ys_ref, yl_ref, ym_ref, w_ref, x_ref, lnw_ref, lnb_ref, of_ref, ob_ref):
    k = pl.program_id(1)

    @pl.when(k == 0)
    def _():
        of_ref[...] = _dot(ys_ref[...], w_ref[...])

    @pl.when(jnp.logical_and(k > 0, k < OUT_K_SSD))
    def _():
        of_ref[...] += _dot(ys_ref[...], w_ref[...])

    @pl.when(jnp.logical_and(k >= OUT_K_SSD, k < OUT_K_SSD + OUT_K_LRU))
    def _():
        of_ref[...] += _dot(yl_ref[...], w_ref[...])

    @pl.when(k >= OUT_K_SSD + OUT_K_LRU)
    def _():
        of_ref[...] += _dot(ym_ref[...], w_ref[...])

    xr = pl.ds(pl.multiple_of(k * OUT_X_ROWS, OUT_X_ROWS), OUT_X_ROWS)
    of_ref[xr, :] += DEEPNORM_ALPHA * x_ref[...]

    @pl.when(k == OUT_K_STEPS - 1)
    def _():
        def ln(r, carry):
            r0 = pl.multiple_of(r * LN_ROWS, LN_ROWS)
            v = of_ref[pl.ds(r0, LN_ROWS), :]
            mu = jnp.mean(v, axis=1, keepdims=True)
            vc = v - mu
            var = jnp.mean(vc * vc, axis=1, keepdims=True)
            o = vc * lax.rsqrt(var + 1e-5) * lnw_ref[...] + lnb_ref[...]
            of_ref[pl.ds(r0, LN_ROWS), :] = o
            ob_ref[pl.ds(r0, LN_ROWS), :] = o.astype(BF16)
            return carry

        lax.fori_loop(0, OUT_TM // LN_ROWS, ln, 0, unroll=2)


def _out_ln(y_ssd, y_lru, y_ml, w_out16, x, layer, ln_w, ln_b):
    t = x.shape[0]
    tm, tk = OUT_TM, OUT_TK
    k1, k2 = OUT_K_SSD, OUT_K_SSD + OUT_K_LRU
    vec = pl.BlockSpec((None, 1, D_MODEL), lambda i, k: (layer, 0, 0))
    return pl.pallas_call(
        _out_ln_kernel,
        out_shape=(jax.ShapeDtypeStruct((t, D_MODEL), F32), jax.ShapeDtypeStruct((t, D_MODEL), BF16)),
        grid=(t // tm, OUT_K_STEPS),
        in_specs=[pl.BlockSpec((tm, tk), lambda i, k: (i, jnp.minimum(k, k1 - 1))),
                  pl.BlockSpec((tm, tk), lambda i, k: (i, jnp.clip(k - k1, 0, OUT_K_LRU - 1))),
                  pl.BlockSpec((tm, tk), lambda i, k: (i, jnp.clip(k - k2, 0, OUT_K_ML - 1))),
                  pl.BlockSpec((None, tk, D_MODEL), lambda i, k: (layer, k, 0)),
                  pl.BlockSpec((OUT_X_ROWS, D_MODEL), lambda i, k: (i * OUT_K_STEPS + k, 0)),
                  vec, vec],
        out_specs=(pl.BlockSpec((tm, D_MODEL), lambda i, k: (i, 0)),
                   pl.BlockSpec((tm, D_MODEL), lambda i, k: (i, 0))),
        compiler_params=_params(2),
        name="out_proj_ln",
    )(y_ssd, y_lru, y_ml, w_out16, x, ln_w, ln_b)


def _row3(p):
    return p.reshape(p.shape[0], 1, p.shape[1])


def kernel(x, w_in, ssd_conv_w, ssd_conv_b, ssd_dt_bias, ssd_a_log, ssd_d, ssd_norm_w, lru_conv_w, lru_conv_b,
           lru_w_a, lru_b_a, lru_w_x, lru_b_x, lru_lambda, mlstm_conv_w, mlstm_conv_b, mlstm_w_q, mlstm_w_k,
           mlstm_w_v, mlstm_b_i, mlstm_b_f, mlstm_norm_w, w_out, ln_w, ln_b):
    batch, seq, d = x.shape
    depth = w_in.shape[0]
    assert d == D_MODEL and seq % TIME_BLOCK == 0 and depth == DEPTH and w_in.shape[2] == N_IN
    xf = x.reshape(batch * seq, d)
    x16 = _cast_bf16(xf)

    wt = jnp.swapaxes(w_in, 1, 2)
    n_pad = LANES - SSD_HEADS - 2 * MLSTM_HEADS
    wt_small = jnp.concatenate([wt[:, N_HEAD:N_HEAD + SSD_HEADS], wt[:, N_IN - 2 * MLSTM_HEADS:],
                                jnp.zeros((depth, n_pad, d), F32)], axis=1)
    w_out16 = _cast_bf16(w_out.reshape(depth * MIX_WIDTH, d)).reshape(depth, MIX_WIDTH, d)
    small_bias = _row3(jnp.concatenate([ssd_dt_bias, mlstm_b_i, mlstm_b_f, jnp.zeros((depth, n_pad), F32)], axis=1))
    alog_row = _row3(jnp.concatenate([ssd_a_log, jnp.zeros((depth, LANES - SSD_HEADS), F32)], axis=1))
    d_exp = _row3(jnp.repeat(ssd_d, SSD_HEAD_DIM, axis=1))
    lru_wa16, lru_wx16 = lru_w_a.astype(BF16), lru_w_x.astype(BF16)
    wq16, wk16, wv16 = mlstm_w_q.astype(BF16), mlstm_w_k.astype(BF16), mlstm_w_v.astype(BF16)

    for l in range(depth):
        proj = _in_proj(x16, wt, l)
        small = _in_proj_small(x16, wt_small, l)
        cum, pd, cd, cum_t, pd_t, q1_t = _head_prep(small, small_bias, alog_row, l)
        y_ssd = _ssd_mixer(proj, cum, cd, cum_t, pd_t, q1_t, batch, seq, l, ssd_conv_w, _row3(ssd_conv_b),
                           d_exp, _row3(ssd_norm_w))
        y_lru = _lru_mixer(proj, batch, seq, l, lru_conv_w, _row3(lru_conv_b), lru_wa16, lru_wx16,
                           _row3(lru_b_a), _row3(lru_b_x), _row3(lru_lambda))
        y_ml = _mlstm_mixer(proj, cum, pd, cum_t, pd_t, batch, seq, l, mlstm_conv_w, _row3(mlstm_conv_b),
                            wq16, wk16, wv16, _row3(mlstm_norm_w))
        xf, x16 = _out_ln(y_ssd, y_lru, y_ml, w_out16, xf, l, _row3(ln_w), _row3(ln_b))
    return xf.reshape(batch, seq, d)
```

```python
import jax
import jax.numpy as jnp
from jax import lax
from jax.experimental import pallas as pl
from jax.experimental.pallas import tpu as pltpu

F32 = jnp.float32
BF16 = jnp.bfloat16

D_MODEL = 4096
DEPTH = 2
MIX_WIDTH = 2 * D_MODEL
SSD_WIDTH = MIX_WIDTH // 2
LRU_WIDTH = MIX_WIDTH // 4
MLSTM_WIDTH = MIX_WIDTH - SSD_WIDTH - LRU_WIDTH
SSD_HEAD_DIM = 64
SSD_HEAD_SHIFT = 6
SSD_HEADS = SSD_WIDTH // SSD_HEAD_DIM
SSD_GROUPS = 8
SSD_HPG = SSD_HEADS // SSD_GROUPS
SSD_STATE = 128
SSD_GROUP_WIDTH = SSD_WIDTH // SSD_GROUPS
LRU_BLOCKS = 16
LRU_BLOCK_DIM = LRU_WIDTH // LRU_BLOCKS
LRU_C = 8.0
MLSTM_HEADS = 8
MLSTM_HEAD_DIM = MLSTM_WIDTH // MLSTM_HEADS
CONV_WIDTH = 4
CHUNK = 128
DEEPNORM_ALPHA = (2.0 * DEPTH) ** 0.25

SUBLANES = 8
LANES = 128

XBC_WIDTH = SSD_WIDTH + 2 * SSD_GROUPS * SSD_STATE
N_HEAD = SSD_WIDTH + XBC_WIDTH
N_TAIL = 2 * LRU_WIDTH + 3 * MLSTM_WIDTH
N_MAIN = N_HEAD + N_TAIL
N_IN = N_HEAD + SSD_HEADS + N_TAIL + 2 * MLSTM_HEADS
COL_Z = 0
COL_XS = SSD_WIDTH
COL_B = COL_XS + SSD_WIDTH
COL_C = COL_B + SSD_GROUPS * SSD_STATE
COL_LRU_X = COL_C + SSD_GROUPS * SSD_STATE
COL_LRU_G = COL_LRU_X + LRU_WIDTH
COL_ML_X = COL_LRU_G + LRU_WIDTH
COL_ML_O = COL_ML_X + MLSTM_WIDTH
COL_ML_G = COL_ML_O + MLSTM_WIDTH
SMALL_I = SSD_HEADS
SMALL_F = SSD_HEADS + MLSTM_HEADS

VMEM_LIMIT = 56 * 1024 * 1024
TIME_BLOCK = 2048
CHUNKS_PER_BLOCK = TIME_BLOCK // CHUNK


def _params(n_axes):
    return pltpu.CompilerParams(dimension_semantics=("arbitrary",) * n_axes, vmem_limit_bytes=VMEM_LIMIT)


def _sigmoid(x):
    return 0.5 * jnp.tanh(0.5 * x) + 0.5


def _silu(x):
    h = 0.5 * x
    return h + h * jnp.tanh(h)


def _softplus(x):
    return jnp.maximum(x, 0.0) + jnp.log1p(jnp.exp(-jnp.abs(x)))


def _dot(a, b):
    return jnp.dot(a, b, preferred_element_type=F32)


def _dot_nt(a, b):
    return lax.dot_general(a, b, (((1,), (1,)), ((), ())), preferred_element_type=F32)


def _dot_f32(a, b):
    return jnp.dot(a, b, precision=lax.Precision.HIGHEST, preferred_element_type=F32)


def _iota(shape, dim):
    return lax.broadcasted_iota(jnp.int32, shape, dim)


def _column(x, idx):
    lane = _iota(x.shape, 1)
    return jnp.sum(jnp.where(lane == idx, x, 0.0), axis=1, keepdims=True)


def _conv_block(x_ref, xp_scr, w_ref, b_ref, out_scr, first, width, apply_silu):
    tb = TIME_BLOCK

    @pl.when(first)
    def _():
        xp_scr[0:SUBLANES, :] = jnp.zeros((SUBLANES, width), F32)

    @pl.when(jnp.logical_not(first))
    def _():
        xp_scr[0:SUBLANES, :] = xp_scr[tb:tb + SUBLANES, :]

    xp_scr[SUBLANES:SUBLANES + tb, :] = x_ref[...]
    for c in range(CHUNKS_PER_BLOCK):
        r0 = c * CHUNK
        for l0 in range(0, width, LANES):
            acc = b_ref[:, l0:l0 + LANES] + w_ref[3:4, l0:l0 + LANES] * xp_scr[r0 + 8:r0 + 8 + CHUNK, l0:l0 + LANES]
            for k in range(CONV_WIDTH - 1):
                off = r0 + 8 - (CONV_WIDTH - 1) + k
                acc = acc + w_ref[k:k + 1, l0:l0 + LANES] * xp_scr[off:off + CHUNK, l0:l0 + LANES]
            if apply_silu:
                acc = _silu(acc)
            out_scr[r0:r0 + CHUNK, l0:l0 + LANES] = acc


CAST_ROWS = 512


def _cast_kernel(a_ref, o_ref):
    o_ref[...] = a_ref[...].astype(BF16)


def _cast_bf16(a):
    r, c = a.shape
    return pl.pallas_call(
        _cast_kernel,
        out_shape=jax.ShapeDtypeStruct((r, c), BF16),
        grid=(r // CAST_ROWS,),
        in_specs=[pl.BlockSpec((CAST_ROWS, c), lambda i: (i, 0))],
        out_specs=pl.BlockSpec((CAST_ROWS, c), lambda i: (i, 0)),
        compiler_params=_params(1),
        name="cast_bf16",
    )(a)


IN_TM = 2048
IN_TN = 512
IN_SMALL_TM = 1024


def _in_proj_kernel(a_ref, wt_ref, ws_ref, o_ref, s_ref):
    o_ref[...] = _dot_nt(a_ref[...], wt_ref[...].astype(BF16))

    @pl.when(pl.program_id(1) == 0)
    def _():
        s_ref[...] = _dot_nt(a_ref[...], ws_ref[...].astype(BF16))


def _in_proj(x16, wt_all, wt_small, layer):
    m, k = x16.shape
    head_blocks = N_HEAD // IN_TN

    def w_index(i, j):
        row = j * IN_TN + jnp.where(j >= head_blocks, SSD_HEADS, 0)
        return (layer, pl.multiple_of(row, SSD_HEADS), 0)

    return pl.pallas_call(
        _in_proj_kernel,
        out_shape=(jax.ShapeDtypeStruct((m, N_MAIN), F32), jax.ShapeDtypeStruct((m, LANES), F32)),
        grid=(m // IN_TM, N_MAIN // IN_TN),
        in_specs=[pl.BlockSpec((IN_TM, k), lambda i, j: (i, 0), pipeline_mode=pl.Buffered(1)),
                  pl.BlockSpec((None, pl.Element(IN_TN), pl.Element(k)), w_index),
                  pl.BlockSpec((None, LANES, k), lambda i, j: (layer, 0, 0), pipeline_mode=pl.Buffered(1))],
        out_specs=(pl.BlockSpec((IN_TM, IN_TN), lambda i, j: (i, j)),
                   pl.BlockSpec((IN_TM, LANES), lambda i, j: (i, 0))),
        compiler_params=_params(2),
        name="in_proj",
    )(x16, wt_all, wt_small)


def _head_prep_kernel(sm_ref, bias_ref, alog_ref, cum_ref, pd_ref, cd_ref, cum_t_ref, pd_t_ref, q1_t_ref):
    ltri = (_iota((CHUNK, CHUNK), 1) <= _iota((CHUNK, CHUNK), 0)).astype(F32)
    lane = _iota((1, LANES), 1)
    is_ssd = lane < SSD_HEADS
    a_neg = jnp.where(is_ssd, -jnp.exp(alog_ref[...]), 0.0)
    chunk_decay = []
    for c in range(CHUNKS_PER_BLOCK):
        r0 = c * CHUNK
        pre = sm_ref[r0:r0 + CHUNK, :] + bias_ref[...]
        dt = _softplus(pre)
        log_f = -_softplus(-pre)
        cum = _dot_f32(ltri, jnp.where(is_ssd, dt * a_neg, log_f))
        pd = jnp.where(is_ssd, dt, pre)
        cum_ref[r0:r0 + CHUNK, :] = cum
        pd_ref[r0:r0 + CHUNK, :] = pd
        cum_last = cum[CHUNK - 1:CHUNK, :]
        cum_t_ref[:, r0:r0 + CHUNK] = cum.T
        pd_t_ref[:, r0:r0 + CHUNK] = pd.T
        q1_t_ref[:, r0:r0 + CHUNK] = (jnp.exp(cum_last - cum) * dt).T
        chunk_decay.append(jnp.exp(cum_last))
    cd = jnp.concatenate(chunk_decay, axis=0)
    cd_hi = cd.astype(BF16)
    cd_lo = (cd - cd_hi.astype(F32)).astype(BF16)
    expand = (jnp.right_shift(_iota((LANES, SSD_WIDTH), 1), SSD_HEAD_SHIFT)
              == _iota((LANES, SSD_WIDTH), 0)).astype(F32).astype(BF16)
    cd_ref[...] = _dot(cd_hi, expand) + _dot(cd_lo, expand)


def _head_prep(small, bias_all, alog_all, layer):
    t = small.shape[0]
    tb = TIME_BLOCK
    nb = t // tb
    row_blk = pl.BlockSpec((tb, LANES), lambda i: (i, 0))
    vec = pl.BlockSpec((None, 1, LANES), lambda i: (layer, 0, 0))
    t_blk = pl.BlockSpec((None, LANES, tb), lambda i: (i, 0, 0))
    flat = jax.ShapeDtypeStruct((t, LANES), F32)
    transposed = jax.ShapeDtypeStruct((nb, LANES, tb), F32)
    cd_shape = jax.ShapeDtypeStruct((t // CHUNK, SSD_WIDTH), F32)
    cd_blk = pl.BlockSpec((CHUNKS_PER_BLOCK, SSD_WIDTH), lambda i: (i, 0))
    return pl.pallas_call(
        _head_prep_kernel,
        out_shape=(flat, flat, cd_shape, transposed, transposed, transposed),
        grid=(nb,),
        in_specs=[row_blk, vec, vec],
        out_specs=(row_blk, row_blk, cd_blk, t_blk, t_blk, t_blk),
        compiler_params=_params(1),
        name="head_prep",
    )(small, bias_all, alog_all)


def _ssd_kernel(z_ref, xs_ref, bm_ref, cm_ref, cum_ref, cd_ref, cum_t_ref, dt_t_ref, q1_t_ref,
                cwx_ref, cwb_ref, cwc_ref, cbx_ref, cbb_ref, cbc_ref, dexp_ref, nw_ref,
                y_ref,
                xpx_scr, xpb_scr, xpc_scr, xcx_scr, xcb_scr, xcc_scr, st_scr):
    g = pl.program_id(1)
    first = pl.program_id(2) == 0
    gw = SSD_GROUP_WIDTH

    @pl.when(first)
    def _():
        st_scr[...] = jnp.zeros(st_scr.shape, F32)

    _conv_block(xs_ref, xpx_scr, cwx_ref, cbx_ref, xcx_scr, first, gw, True)
    _conv_block(bm_ref, xpb_scr, cwb_ref, cbb_ref, xcb_scr, first, SSD_STATE, True)
    _conv_block(cm_ref, xpc_scr, cwc_ref, cbc_ref, xcc_scr, first, SSD_STATE, True)

    causal = _iota((CHUNK, CHUNK), 1) <= _iota((CHUNK, CHUNK), 0)
    low_half = _iota((CHUNK, LANES), 1) < SSD_HEAD_DIM
    prev = st_scr[...]

    for c in range(CHUNKS_PER_BLOCK):
        r0 = c * CHUNK
        xs = xcx_scr[r0:r0 + CHUNK, :]
        bm = xcb_scr[r0:r0 + CHUNK, :]
        cm16 = xcc_scr[r0:r0 + CHUNK, :].astype(BF16)
        acum_all = cum_ref[r0:r0 + CHUNK, :]
        acum_t = cum_t_ref[:, r0:r0 + CHUNK]
        dt_t = dt_t_ref[:, r0:r0 + CHUNK]
        q1_t = q1_t_ref[:, r0:r0 + CHUNK]

        cb = _dot_nt(cm16, bm.astype(BF16))
        bm_t = bm.T
        c_prev =_dot(cm16, prev.astype(BF16))

        ys, sts = [], []
        for j in range(SSD_HPG // 2):
            pair = slice(j * LANES, (j + 1) * LANES)
            wts, bts, q2 = [], [], []
            for r in (2 * j, 2 * j + 1):
                acol = _column(acum_all, g * SSD_HPG + r)
                seg = acol - acum_t[r:r + 1, :]
                decay = jnp.exp(jnp.where(causal, seg, -jnp.inf))
                wts.append((cb * decay * dt_t[r:r + 1, :]).astype(BF16))
                bts.append((bm_t * q1_t[r:r + 1, :]).astype(BF16))
                q2.append(jnp.exp(acol))
            xp = xs[:, pair]
            rhs = jnp.concatenate([jnp.where(low_half, xp, 0.0).astype(BF16),
                                   jnp.where(low_half, 0.0, xp).astype(BF16)], axis=0)
            y_diag = _dot(jnp.concatenate(wts, axis=1), rhs)
            sts.append(_dot(jnp.concatenate(bts, axis=1), rhs))
            ys.append(y_diag + c_prev[:, pair] * jnp.where(low_half, q2[0], q2[1]))
        y = jnp.concatenate(ys, axis=1)
        prev = prev * cd_ref[c:c + 1, :] + jnp.concatenate(sts, axis=1)

        y = y + dexp_ref[...] * xs
        y = y * _silu(z_ref[r0:r0 + CHUNK, :])
        y = y * lax.rsqrt(jnp.mean(y * y, axis=1, keepdims=True) + 1e-6)
        y_ref[r0:r0 + CHUNK, :] = (y * nw_ref[...]).astype(BF16)

    st_scr[...] = prev


def _ssd_mixer(proj, cum, cd, cum_t, pd_t, q1_t, batch, seq, layer, conv_w, conv_b, d_exp, norm_w):
    t = proj.shape[0]
    nb = seq // TIME_BLOCK
    tb = TIME_BLOCK
    gw = SSD_GROUP_WIDTH

    def col_spec(width, col0):
        base = col0 // width
        return pl.BlockSpec((tb, width), lambda b, g, i: (b * nb + i, base + g))

    def w_spec(nrows, width, col0):
        base = col0 // width
        return pl.BlockSpec((None, nrows, width), lambda b, g, i: (layer, 0, base + g))

    seq_blk = pl.BlockSpec((tb, LANES), lambda b, g, i: (b * nb + i, 0))
    t_blk = pl.BlockSpec((None, SSD_HPG, tb), lambda b, g, i: (b * nb + i, g, 0))
    cd_blk = pl.BlockSpec((CHUNKS_PER_BLOCK, gw), lambda b, g, i: (b * nb + i, g))
    cx, cb_, cc = 0, SSD_WIDTH, SSD_WIDTH + SSD_GROUPS * SSD_STATE
    return pl.pallas_call(
        _ssd_kernel,
        out_shape=jax.ShapeDtypeStruct((t, SSD_WIDTH), BF16),
        grid=(batch, SSD_GROUPS, nb),
        in_specs=[col_spec(gw, COL_Z), col_spec(gw, COL_XS), col_spec(SSD_STATE, COL_B),
                  col_spec(SSD_STATE, COL_C),
                  seq_blk, cd_blk, t_blk, t_blk, t_blk,
                  w_spec(CONV_WIDTH, gw, cx), w_spec(CONV_WIDTH, SSD_STATE, cb_),
                  w_spec(CONV_WIDTH, SSD_STATE, cc),
                  w_spec(1, gw, cx), w_spec(1, SSD_STATE, cb_), w_spec(1, SSD_STATE, cc),
                  w_spec(1, gw, 0), w_spec(1, gw, 0)],
        out_specs=pl.BlockSpec((tb, gw), lambda b, g, i: (b * nb + i, g)),
        scratch_shapes=[pltpu.VMEM((tb + 2 * SUBLANES, gw), F32),
                        pltpu.VMEM((tb + 2 * SUBLANES, SSD_STATE), F32),
                        pltpu.VMEM((tb + 2 * SUBLANES, SSD_STATE), F32),
                        pltpu.VMEM((tb, gw), F32),
                        pltpu.VMEM((tb, SSD_STATE), F32),
                        pltpu.VMEM((tb, SSD_STATE), F32),
                        pltpu.VMEM((SSD_STATE, gw), F32)],
        compiler_params=_params(3),
        name="ssd_mixer",
    )(proj, proj, proj, proj, cum, cd, cum_t, pd_t, q1_t,
      conv_w, conv_w, conv_w, conv_b, conv_b, conv_b, d_exp, norm_w)


def _lru_kernel(x_ref, gate_ref, cw_ref, cb_ref, wa_ref, wx_ref, ba_ref, bx_ref, lam_ref,
                y_ref, xp_scr, xc_scr, h_scr):
    first = pl.program_id(2) == 0

    @pl.when(first)
    def _():
        h_scr[...] = jnp.zeros(h_scr.shape, F32)

    _conv_block(x_ref, xp_scr, cw_ref, cb_ref, xc_scr, first, LRU_BLOCK_DIM, False)
    row_in_tile = _iota((CHUNK // SUBLANES, SUBLANES, LANES), 1)
    keep = {d: row_in_tile >= d for d in (1, 2, 4)}
    sp_lam = _softplus(-lam_ref[...])
    carry = h_scr[...]

    for c in range(CHUNKS_PER_BLOCK):
        r0 = c * CHUNK
        xc = xc_scr[r0:r0 + CHUNK, :]
        xc16 = xc.astype(BF16)
        r = _sigmoid(_dot(xc16, wa_ref[0]) + ba_ref[...])
        i = _sigmoid(_dot(xc16, wx_ref[0]) + bx_ref[...])
        log_a = -LRU_C * r * sp_lam
        a = jnp.exp(log_a)
        var = -jnp.tanh(log_a) * (a * a + 1.0)
        u = jnp.where(var > 0.0, var * lax.rsqrt(var), 0.0) * (i * xc)
        tiles = CHUNK // SUBLANES
        a = a.reshape(tiles, SUBLANES, LANES)
        u = u.reshape(tiles, SUBLANES, LANES)
        d = 1
        while d < SUBLANES:
            a_sh = pltpu.roll(a, d, 1)
            u_sh = pltpu.roll(u, d, 1)
            u = jnp.where(keep[d], u + a * u_sh, u)
            a = jnp.where(keep[d], a * a_sh, a)
            d *= 2
        hs = []
        for t in range(tiles):
            hs.append(u[t] + a[t] * carry)
            carry = hs[-1][SUBLANES - 1:SUBLANES, :]
        h = jnp.concatenate(hs, axis=0)
        y_ref[r0:r0 + CHUNK, :] = (h * _silu(gate_ref[r0:r0 + CHUNK, :])).astype(BF16)

    h_scr[...] = carry


def _lru_mixer(proj, batch, seq, layer, conv_w, conv_b, w_a, w_x, b_a, b_x, lam):
    t = proj.shape[0]
    nb = seq // TIME_BLOCK
    tb = TIME_BLOCK
    bd = LRU_BLOCK_DIM

    def col_spec(col0):
        base = col0 // bd
        return pl.BlockSpec((tb, bd), lambda b, h, i: (b * nb + i, base + h))

    vec = pl.BlockSpec((None, 1, bd), lambda b, h, i: (layer, 0, h))
    wspec = pl.BlockSpec((None, 1, bd, bd), lambda b, h, i: (layer, h, 0, 0))
    return pl.pallas_call(
        _lru_kernel,
        out_shape=jax.ShapeDtypeStruct((t, LRU_WIDTH), BF16),
        grid=(batch, LRU_BLOCKS, nb),
        in_specs=[col_spec(COL_LRU_X), col_spec(COL_LRU_G),
                  pl.BlockSpec((None, CONV_WIDTH, bd), lambda b, h, i: (layer, 0, h)), vec,
                  wspec, wspec, vec, vec, vec],
        out_specs=pl.BlockSpec((tb, bd), lambda b, h, i: (b * nb + i, h)),
        scratch_shapes=[pltpu.VMEM((tb + 2 * SUBLANES, bd), F32),
                        pltpu.VMEM((tb, bd), F32),
                        pltpu.VMEM((1, bd), F32)],
        compiler_params=_params(3),
        name="lru_mixer",
    )(proj, proj, conv_w, conv_b, w_a, w_x, b_a, b_x, lam)


def _mlstm_kernel(x_ref, o_ref, gate_ref, cum_ref, pd_ref, cum_t_ref, pd_t_ref, cw_ref, cb_ref,
                  wq_ref, wk_ref, wv_ref, nw_ref,
                  y_ref, xp_scr, xc_scr, c_scr, n_scr, m_scr):
    h = pl.program_id(1)
    first = pl.program_id(2) == 0
    dh = MLSTM_HEAD_DIM

    @pl.when(first)
    def _():
        c_scr[...] = jnp.zeros(c_scr.shape, F32)
        n_scr[...] = jnp.zeros(n_scr.shape, F32)
        m_scr[...] = jnp.zeros(m_scr.shape, F32)

    _conv_block(x_ref, xp_scr, cw_ref, cb_ref, xc_scr, first, dh, True)
    causal = _iota((CHUNK, CHUNK), 1) <= _iota((CHUNK, CHUNK), 0)
    is_head = _iota((MLSTM_HEADS, CHUNK), 0) == h
    c_st = c_scr[...]
    n_st = n_scr[...]
    m_prev = m_scr[...]

    for c in range(CHUNKS_PER_BLOCK):
        r0 = c * CHUNK
        xm16 = x_ref[r0:r0 + CHUNK, :].astype(BF16)
        xc16 = xc_scr[r0:r0 + CHUNK, :].astype(BF16)
        q = _dot(xc16, wq_ref[0])
        k = _dot(xc16, wk_ref[0]) * (dh ** -0.5)
        v = _dot(xm16, wv_ref[0])
        q16 = q.astype(BF16)
        v16 = v.astype(BF16)

        i_row = jnp.sum(jnp.where(is_head, pd_t_ref[:, r0:r0 + CHUNK], 0.0), axis=0, keepdims=True)
        b_row = jnp.sum(jnp.where(is_head, cum_t_ref[:, r0:r0 + CHUNK], 0.0), axis=0, keepdims=True)
        i_col = _column(pd_ref[r0:r0 + CHUNK, :], SMALL_I + h)
        b_col = _column(cum_ref[r0:r0 + CHUNK, :], SMALL_F + h)

        a_inter = b_col + m_prev
        d_intra = jnp.where(causal, b_col - b_row + i_row, -jnp.inf)
        m_t = jnp.maximum(a_inter, jnp.max(d_intra, axis=1, keepdims=True))
        w_intra = jnp.exp(d_intra - m_t)
        w_inter = jnp.exp(a_inter - m_t)
        s = _dot_nt(q16, k.astype(BF16)) * w_intra
        num =_dot(s.astype(BF16), v16) + w_inter * _dot(q16, c_st.astype(BF16))
        den = jnp.sum(s, axis=1, keepdims=True) + w_inter * jnp.sum(q * n_st, axis=1, keepdims=True)
        hs = num * (1.0 / jnp.maximum(jnp.abs(den), jnp.exp(-m_t)))

        m_new = m_t[CHUNK - 1:CHUNK, :]
        b_last = b_col[CHUNK - 1:CHUNK, :]
        w_s = jnp.exp(b_last - b_col + i_col - m_new)
        carry_decay = jnp.exp(b_last + m_prev - m_new)
        kw = k * w_s
        c_st = carry_decay * c_st + _dot(kw.T.astype(BF16), v16)
        n_st = carry_decay * n_st + jnp.sum(kw, axis=0, keepdims=True)
        m_prev = m_new

        hn = hs * lax.rsqrt(jnp.mean(hs * hs, axis=1, keepdims=True) + 1e-6) * nw_ref[...]
        out = hn * _sigmoid(o_ref[r0:r0 + CHUNK, :]) * _silu(gate_ref[r0:r0 + CHUNK, :])
        y_ref[r0:r0 + CHUNK, :] = out.astype(BF16)

    c_scr[...] = c_st
    n_scr[...] = n_st
    m_scr[...] = m_prev


def _mlstm_mixer(proj, cum, pd, cum_t, pd_t, batch, seq, layer, conv_w, conv_b, w_q, w_k, w_v, norm_w):
    t = proj.shape[0]
    nb = seq // TIME_BLOCK
    tb = TIME_BLOCK
    dh = MLSTM_HEAD_DIM

    def col_spec(col0):
        base = col0 // dh
        return pl.BlockSpec((tb, dh), lambda b, h, i: (b * nb + i, base + h))

    def t_blk(row0):
        return pl.BlockSpec((None, MLSTM_HEADS, tb), lambda b, h, i: (b * nb + i, row0 // MLSTM_HEADS, 0))

    seq_blk = pl.BlockSpec((tb, LANES), lambda b, h, i: (b * nb + i, 0))
    vec = pl.BlockSpec((None, 1, dh), lambda b, h, i: (layer, 0, h))
    wspec = pl.BlockSpec((None, 1, dh, dh), lambda b, h, i: (layer, h, 0, 0))
    return pl.pallas_call(
        _mlstm_kernel,
        out_shape=jax.ShapeDtypeStruct((t, MLSTM_WIDTH), BF16),
        grid=(batch, MLSTM_HEADS, nb),
        in_specs=[col_spec(COL_ML_X), col_spec(COL_ML_O), col_spec(COL_ML_G),
                  seq_blk, seq_blk, t_blk(SMALL_F), t_blk(SMALL_I),
                  pl.BlockSpec((None, CONV_WIDTH, dh), lambda b, h, i: (layer, 0, h)), vec,
                  wspec, wspec, wspec, vec],
        out_specs=pl.BlockSpec((tb, dh), lambda b, h, i: (b * nb + i, h)),
        scratch_shapes=[pltpu.VMEM((tb + 2 * SUBLANES, dh), F32),
                        pltpu.VMEM((tb, dh), F32),
                        pltpu.VMEM((dh, dh), F32),
                        pltpu.VMEM((1, dh), F32),
                        pltpu.VMEM((1, 1), F32)],
        compiler_params=_params(3),
        name="mlstm_mixer",
    )(proj, proj, proj, cum, pd, cum_t, pd_t, conv_w, conv_b, w_q, w_k, w_v, norm_w)


OUT_TM = 512
OUT_TK = 1024
OUT_K_SSD = SSD_WIDTH // OUT_TK
OUT_K_LRU = LRU_WIDTH // OUT_TK
OUT_K_ML = MLSTM_WIDTH // OUT_TK
OUT_K_STEPS = OUT_K_SSD + OUT_K_LRU + OUT_K_ML
OUT_X_ROWS = OUT_TM // OUT_K_STEPS
LN_ROWS = 64


def _out_ln_kernel(ys_ref, yl_ref, ym_ref, w_ref, x_ref, lnw_ref, lnb_ref, of_ref, ob_ref):
    k = pl.program_id(1)

    @pl.when(k == 0)
    def _():
        of_ref[...] = _dot(ys_ref[...], w_ref[...])

    @pl.when(jnp.logical_and(k > 0, k < OUT_K_SSD))
    def _():
        of_ref[...] += _dot(ys_ref[...], w_ref[...])

    @pl.when(jnp.logical_and(k >= OUT_K_SSD, k < OUT_K_SSD + OUT_K_LRU))
    def _():
        of_ref[...] += _dot(yl_ref[...], w_ref[...])

    @pl.when(k >= OUT_K_SSD + OUT_K_LRU)
    def _():
        of_ref[...] += _dot(ym_ref[...], w_ref[...])

    xr = pl.ds(pl.multiple_of(k * OUT_X_ROWS, OUT_X_ROWS), OUT_X_ROWS)
    of_ref[xr, :] += DEEPNORM_ALPHA * x_ref[...]

    @pl.when(k == OUT_K_STEPS - 1)
    def _():
        def ln(r, carry):
            r0 = pl.multiple_of(r * LN_ROWS, LN_ROWS)
            rows = pl.ds(r0, LN_ROWS)
            mu = jnp.mean(of_ref[rows, :], axis=1, keepdims=True)
            vc = of_ref[rows, :] - mu
            var = jnp.mean(vc * vc, axis=1, keepdims=True)
            o = (of_ref[rows, :] - mu) * lax.rsqrt(var + 1e-5) * lnw_ref[...] + lnb_ref[...]
            of_ref[pl.ds(r0, LN_ROWS), :] = o
            ob_ref[pl.ds(r0, LN_ROWS), :] = o.astype(BF16)
            return carry

        lax.fori_loop(0, OUT_TM // LN_ROWS, ln, 0, unroll=2)


def _out_ln(y_ssd, y_lru, y_ml, w_out16, x, layer, ln_w, ln_b):
    t = x.shape[0]
    tm, tk = OUT_TM, OUT_TK
    k1, k2 = OUT_K_SSD, OUT_K_SSD + OUT_K_LRU
    vec = pl.BlockSpec((None, 1, D_MODEL), lambda i, k: (layer, 0, 0))
    return pl.pallas_call(
        _out_ln_kernel,
        out_shape=(jax.ShapeDtypeStruct((t, D_MODEL), F32), jax.ShapeDtypeStruct((t, D_MODEL), BF16)),
        grid=(t // tm, OUT_K_STEPS),
        in_specs=[pl.BlockSpec((tm, tk), lambda i, k: (i, jnp.minimum(k, k1 - 1))),
                  pl.BlockSpec((tm, tk), lambda i, k: (i, jnp.clip(k - k1, 0, OUT_K_LRU - 1))),
                  pl.BlockSpec((tm, tk), lambda i, k: (i, jnp.clip(k - k2, 0, OUT_K_ML - 1))),
                  pl.BlockSpec((None, tk, D_MODEL), lambda i, k: (layer, k, 0)),
                  pl.BlockSpec((OUT_X_ROWS, D_MODEL), lambda i, k: (i * OUT_K_STEPS + k, 0)),
                  vec, vec],
        out_specs=(pl.BlockSpec((tm, D_MODEL), lambda i, k: (i, 0)),
                   pl.BlockSpec((tm, D_MODEL), lambda i, k: (i, 0))),
        compiler_params=_params(2),
        name="out_proj_ln",
    )(y_ssd, y_lru, y_ml, w_out16, x, ln_w, ln_b)


def _row3(p):
    return p.reshape(p.shape[0], 1, p.shape[1])


def kernel(x, w_in, ssd_conv_w, ssd_conv_b, ssd_dt_bias, ssd_a_log, ssd_d, ssd_norm_w, lru_conv_w, lru_conv_b,
           lru_w_a, lru_b_a, lru_w_x, lru_b_x, lru_lambda, mlstm_conv_w, mlstm_conv_b, mlstm_w_q, mlstm_w_k,
           mlstm_w_v, mlstm_b_i, mlstm_b_f, mlstm_norm_w, w_out, ln_w, ln_b):
    batch, seq, d = x.shape
    depth = w_in.shape[0]
    assert d == D_MODEL and seq % TIME_BLOCK == 0 and depth == DEPTH and w_in.shape[2] == N_IN
    xf = x.reshape(batch * seq, d)
    x16 = _cast_bf16(xf)

    wt = jnp.swapaxes(w_in, 1, 2)
    n_pad = LANES - SSD_HEADS - 2 * MLSTM_HEADS
    wt_small = jnp.concatenate([wt[:, N_HEAD:N_HEAD + SSD_HEADS], wt[:, N_IN - 2 * MLSTM_HEADS:],
                                jnp.zeros((depth, n_pad, d), F32)], axis=1)
    w_out16 = _cast_bf16(w_out.reshape(depth * MIX_WIDTH, d)).reshape(depth, MIX_WIDTH, d)
    small_bias = _row3(jnp.concatenate([ssd_dt_bias, mlstm_b_i, mlstm_b_f, jnp.zeros((depth, n_pad), F32)], axis=1))
    alog_row = _row3(jnp.concatenate([ssd_a_log, jnp.zeros((depth, LANES - SSD_HEADS), F32)], axis=1))
    d_exp = _row3(jnp.repeat(ssd_d, SSD_HEAD_DIM, axis=1))
    lru_wa16, lru_wx16 = lru_w_a.astype(BF16), lru_w_x.astype(BF16)
    wq16, wk16, wv16 = mlstm_w_q.astype(BF16), mlstm_w_k.astype(BF16), mlstm_w_v.astype(BF16)

    for l in range(depth):
        proj, small = _in_proj(x16, wt, wt_small, l)
        cum, pd, cd, cum_t, pd_t, q1_t = _head_prep(small, small_bias, alog_row, l)
        y_ssd = _ssd_mixer(proj, cum, cd, cum_t, pd_t, q1_t, batch, seq, l, ssd_conv_w, _row3(ssd_conv_b),
                           d_exp, _row3(ssd_norm_w))
        y_lru = _lru_mixer(proj, batch, seq, l, lru_conv_w, _row3(lru_conv_b), lru_wa16, lru_wx16,
                           _row3(lru_b_a), _row3(lru_b_x), _row3(lru_lambda))
        y_ml = _mlstm_mixer(proj, cum, pd, cum_t, pd_t, batch, seq, l, mlstm_conv_w, _row3(mlstm_conv_b),
                            wq16, wk16, wv16, _row3(mlstm_norm_w))
        xf, x16 = _out_ln(y_ssd, y_lru, y_ml, w_out16, xf, l, _row3(ln_w), _row3(ln_b))
    return xf.reshape(batch, seq, d)
```

```python
import jax
import jax.numpy as jnp
from jax import lax
from jax.experimental import pallas as pl
from jax.experimental.pallas import tpu as pltpu

F32 = jnp.float32
BF16 = jnp.bfloat16

D_MODEL = 4096
DEPTH = 2
MIX_WIDTH = 2 * D_MODEL
SSD_WIDTH = MIX_WIDTH // 2
LRU_WIDTH = MIX_WIDTH // 4
MLSTM_WIDTH = MIX_WIDTH - SSD_WIDTH - LRU_WIDTH
SSD_HEAD_DIM = 64
SSD_HEAD_SHIFT = 6
SSD_HEADS = SSD_WIDTH // SSD_HEAD_DIM
SSD_GROUPS = 8
SSD_HPG = SSD_HEADS // SSD_GROUPS
SSD_STATE = 128
SSD_GROUP_WIDTH = SSD_WIDTH // SSD_GROUPS
LRU_BLOCKS = 16
LRU_BLOCK_DIM = LRU_WIDTH // LRU_BLOCKS
LRU_C = 8.0
MLSTM_HEADS = 8
MLSTM_HEAD_DIM = MLSTM_WIDTH // MLSTM_HEADS
CONV_WIDTH = 4
CHUNK = 128
DEEPNORM_ALPHA = (2.0 * DEPTH) ** 0.25

SUBLANES = 8
LANES = 128

XBC_WIDTH = SSD_WIDTH + 2 * SSD_GROUPS * SSD_STATE
N_HEAD = SSD_WIDTH + XBC_WIDTH
N_TAIL = 2 * LRU_WIDTH + 3 * MLSTM_WIDTH
N_MAIN = N_HEAD + N_TAIL
N_IN = N_HEAD + SSD_HEADS + N_TAIL + 2 * MLSTM_HEADS
COL_Z = 0
COL_XS = SSD_WIDTH
COL_B = COL_XS + SSD_WIDTH
COL_C = COL_B + SSD_GROUPS * SSD_STATE
COL_LRU_X = COL_C + SSD_GROUPS * SSD_STATE
COL_LRU_G = COL_LRU_X + LRU_WIDTH
COL_ML_X = COL_LRU_G + LRU_WIDTH
COL_ML_O = COL_ML_X + MLSTM_WIDTH
COL_ML_G = COL_ML_O + MLSTM_WIDTH
SMALL_I = SSD_HEADS
SMALL_F = SSD_HEADS + MLSTM_HEADS

VMEM_LIMIT = 56 * 1024 * 1024
TIME_BLOCK = 2048
CHUNKS_PER_BLOCK = TIME_BLOCK // CHUNK


def _params(n_axes):
    return pltpu.CompilerParams(dimension_semantics=("arbitrary",) * n_axes, vmem_limit_bytes=VMEM_LIMIT)


def _sigmoid(x):
    return 0.5 * jnp.tanh(0.5 * x) + 0.5


def _silu(x):
    h = 0.5 * x
    return h + h * jnp.tanh(h)


def _softplus(x):
    return jnp.maximum(x, 0.0) + jnp.log1p(jnp.exp(-jnp.abs(x)))


def _dot(a, b):
    return jnp.dot(a, b, preferred_element_type=F32)


def _dot_nt(a, b):
    return lax.dot_general(a, b, (((1,), (1,)), ((), ())), preferred_element_type=F32)


def _dot_f32(a, b):
    return jnp.dot(a, b, precision=lax.Precision.HIGHEST, preferred_element_type=F32)


def _iota(shape, dim):
    return lax.broadcasted_iota(jnp.int32, shape, dim)


def _column(x, idx):
    lane = _iota(x.shape, 1)
    return jnp.sum(jnp.where(lane == idx, x, 0.0), axis=1, keepdims=True)


def _conv_block(x_ref, xp_scr, w_ref, b_ref, out_scr, first, width, apply_silu):
    tb = TIME_BLOCK

    @pl.when(first)
    def _():
        xp_scr[0:SUBLANES, :] = jnp.zeros((SUBLANES, width), F32)

    @pl.when(jnp.logical_not(first))
    def _():
        xp_scr[0:SUBLANES, :] = xp_scr[tb:tb + SUBLANES, :]

    xp_scr[SUBLANES:SUBLANES + tb, :] = x_ref[...]
    for c in range(CHUNKS_PER_BLOCK):
        r0 = c * CHUNK
        for l0 in range(0, width, LANES):
            acc = b_ref[:, l0:l0 + LANES] + w_ref[3:4, l0:l0 + LANES] * xp_scr[r0 + 8:r0 + 8 + CHUNK, l0:l0 + LANES]
            for k in range(CONV_WIDTH - 1):
                off = r0 + 8 - (CONV_WIDTH - 1) + k
                acc = acc + w_ref[k:k + 1, l0:l0 + LANES] * xp_scr[off:off + CHUNK, l0:l0 + LANES]
            if apply_silu:
                acc = _silu(acc)
            out_scr[r0:r0 + CHUNK, l0:l0 + LANES] = acc


CAST_ROWS = 512


def _cast_kernel(a_ref, o_ref):
    o_ref[...] = a_ref[...].astype(BF16)


def _cast_bf16(a):
    r, c = a.shape
    return pl.pallas_call(
        _cast_kernel,
        out_shape=jax.ShapeDtypeStruct((r, c), BF16),
        grid=(r // CAST_ROWS,),
        in_specs=[pl.BlockSpec((CAST_ROWS, c), lambda i: (i, 0))],
        out_specs=pl.BlockSpec((CAST_ROWS, c), lambda i: (i, 0)),
        compiler_params=_params(1),
        name="cast_bf16",
    )(a)


IN_TM = 2048
IN_TN = 512
IN_SMALL_TM = 1024


def _in_proj_kernel(a_ref, wt_ref, ws_ref, o_ref, s_ref):
    o_ref[...] = _dot_nt(a_ref[...], wt_ref[...].astype(BF16))

    @pl.when(pl.program_id(1) == 0)
    def _():
        s_ref[...] = _dot_nt(a_ref[...], ws_ref[...].astype(BF16))


def _in_proj(x16, wt_all, wt_small, layer):
    m, k = x16.shape
    head_blocks = N_HEAD // IN_TN

    def w_index(i, j):
        row = j * IN_TN + jnp.where(j >= head_blocks, SSD_HEADS, 0)
        return (layer, pl.multiple_of(row, SSD_HEADS), 0)

    return pl.pallas_call(
        _in_proj_kernel,
        out_shape=(jax.ShapeDtypeStruct((m, N_MAIN), F32), jax.ShapeDtypeStruct((m, LANES), F32)),
        grid=(m // IN_TM, N_MAIN // IN_TN),
        in_specs=[pl.BlockSpec((IN_TM, k), lambda i, j: (i, 0), pipeline_mode=pl.Buffered(1)),
                  pl.BlockSpec((None, pl.Element(IN_TN), pl.Element(k)), w_index),
                  pl.BlockSpec((None, LANES, k), lambda i, j: (layer, 0, 0), pipeline_mode=pl.Buffered(1))],
        out_specs=(pl.BlockSpec((IN_TM, IN_TN), lambda i, j: (i, j)),
                   pl.BlockSpec((IN_TM, LANES), lambda i, j: (i, 0))),
        compiler_params=_params(2),
        name="in_proj",
    )(x16, wt_all, wt_small)


def _head_prep_kernel(sm_ref, bias_ref, alog_ref, cum_ref, pd_ref, cd_ref, cum_t_ref, pd_t_ref, q1_t_ref):
    ltri = (_iota((CHUNK, CHUNK), 1) <= _iota((CHUNK, CHUNK), 0)).astype(F32)
    lane = _iota((1, LANES), 1)
    is_ssd = lane < SSD_HEADS
    a_neg = jnp.where(is_ssd, -jnp.exp(alog_ref[...]), 0.0)
    chunk_decay = []
    for c in range(CHUNKS_PER_BLOCK):
        r0 = c * CHUNK
        pre = sm_ref[r0:r0 + CHUNK, :] + bias_ref[...]
        dt = _softplus(pre)
        log_f = -_softplus(-pre)
        cum = _dot_f32(ltri, jnp.where(is_ssd, dt * a_neg, log_f))
        pd = jnp.where(is_ssd, dt, pre)
        cum_ref[r0:r0 + CHUNK, :] = cum
        pd_ref[r0:r0 + CHUNK, :] = pd
        cum_last = cum[CHUNK - 1:CHUNK, :]
        cum_t_ref[:, r0:r0 + CHUNK] = cum.T
        pd_t_ref[:, r0:r0 + CHUNK] = pd.T
        q1_t_ref[:, r0:r0 + CHUNK] = (jnp.exp(cum_last - cum) * dt).T
        chunk_decay.append(jnp.exp(cum_last))
    cd = jnp.concatenate(chunk_decay, axis=0)
    cd_hi = cd.astype(BF16)
    cd_lo = (cd - cd_hi.astype(F32)).astype(BF16)
    expand = (jnp.right_shift(_iota((LANES, SSD_WIDTH), 1), SSD_HEAD_SHIFT)
              == _iota((LANES, SSD_WIDTH), 0)).astype(F32).astype(BF16)
    cd_ref[...] = _dot(cd_hi, expand) + _dot(cd_lo, expand)


def _head_prep(small, bias_all, alog_all, layer):
    t = small.shape[0]
    tb = TIME_BLOCK
    nb = t // tb
    row_blk = pl.BlockSpec((tb, LANES), lambda i: (i, 0))
    vec = pl.BlockSpec((None, 1, LANES), lambda i: (layer, 0, 0))
    t_blk = pl.BlockSpec((None, LANES, tb), lambda i: (i, 0, 0))
    flat = jax.ShapeDtypeStruct((t, LANES), F32)
    transposed = jax.ShapeDtypeStruct((nb, LANES, tb), F32)
    cd_shape = jax.ShapeDtypeStruct((t // CHUNK, SSD_WIDTH), F32)
    cd_blk = pl.BlockSpec((CHUNKS_PER_BLOCK, SSD_WIDTH), lambda i: (i, 0))
    return pl.pallas_call(
        _head_prep_kernel,
        out_shape=(flat, flat, cd_shape, transposed, transposed, transposed),
        grid=(nb,),
        in_specs=[row_blk, vec, vec],
        out_specs=(row_blk, row_blk, cd_blk, t_blk, t_blk, t_blk),
        compiler_params=_params(1),
        name="head_prep",
    )(small, bias_all, alog_all)


def _ssd_kernel(z_ref, xs_ref, bm_ref, cm_ref, cum_ref, cd_ref, cum_t_ref, dt_t_ref, q1_t_ref,
                cwx_ref, cwb_ref, cwc_ref, cbx_ref, cbb_ref, cbc_ref, dexp_ref, nw_ref,
                y_ref,
                xpx_scr, xpb_scr, xpc_scr, xcx_scr, xcb_scr, xcc_scr, st_scr):
    g = pl.program_id(1)
    first = pl.program_id(2) == 0
    gw = SSD_GROUP_WIDTH

    @pl.when(first)
    def _():
        st_scr[...] = jnp.zeros(st_scr.shape, F32)

    _conv_block(xs_ref, xpx_scr, cwx_ref, cbx_ref, xcx_scr, first, gw, True)
    _conv_block(bm_ref, xpb_scr, cwb_ref, cbb_ref, xcb_scr, first, SSD_STATE, True)
    _conv_block(cm_ref, xpc_scr, cwc_ref, cbc_ref, xcc_scr, first, SSD_STATE, True)

    causal = _iota((CHUNK, CHUNK), 1) <= _iota((CHUNK, CHUNK), 0)
    low_half = _iota((CHUNK, LANES), 1) < SSD_HEAD_DIM
    prev = st_scr[...]

    for c in range(CHUNKS_PER_BLOCK):
        r0 = c * CHUNK
        xs = xcx_scr[r0:r0 + CHUNK, :]
        bm = xcb_scr[r0:r0 + CHUNK, :]
        cm16 = xcc_scr[r0:r0 + CHUNK, :].astype(BF16)
        acum_all = cum_ref[r0:r0 + CHUNK, :]
        acum_t = cum_t_ref[:, r0:r0 + CHUNK]
        dt_t = dt_t_ref[:, r0:r0 + CHUNK]
        q1_t = q1_t_ref[:, r0:r0 + CHUNK]

        cb = _dot_nt(cm16, bm.astype(BF16))
        bm_t = bm.T
        c_prev =_dot(cm16, prev.astype(BF16))

        ys, sts = [], []
        for j in range(SSD_HPG // 2):
            pair = slice(j * LANES, (j + 1) * LANES)
            wts, bts, q2 = [], [], []
            for r in (2 * j, 2 * j + 1):
                acol = _column(acum_all, g * SSD_HPG + r)
                seg = acol - acum_t[r:r + 1, :]
                decay = jnp.exp(jnp.where(causal, seg, -jnp.inf))
                wts.append((cb * decay * dt_t[r:r + 1, :]).astype(BF16))
                bts.append((bm_t * q1_t[r:r + 1, :]).astype(BF16))
                q2.append(jnp.exp(acol))
            xp = xs[:, pair]
            rhs = jnp.concatenate([jnp.where(low_half, xp, 0.0).astype(BF16),
                                   jnp.where(low_half, 0.0, xp).astype(BF16)], axis=0)
            y_diag = _dot(jnp.concatenate(wts, axis=1), rhs)
            sts.append(_dot(jnp.concatenate(bts, axis=1), rhs))
            ys.append(y_diag + c_prev[:, pair] * jnp.where(low_half, q2[0], q2[1]))
        y = jnp.concatenate(ys, axis=1)
        prev = prev * cd_ref[c:c + 1, :] + jnp.concatenate(sts, axis=1)

        y = y + dexp_ref[...] * xs
        y = y * _silu(z_ref[r0:r0 + CHUNK, :])
        y = y * lax.rsqrt(jnp.mean(y * y, axis=1, keepdims=True) + 1e-6)
        y_ref[r0:r0 + CHUNK, :] = (y * nw_ref[...]).astype(BF16)

    st_scr[...] = prev


def _ssd_mixer(proj, cum, cd, cum_t, pd_t, q1_t, batch, seq, layer, conv_w, conv_b, d_exp, norm_w):
    t = proj.shape[0]
    nb = seq // TIME_BLOCK
    tb = TIME_BLOCK
    gw = SSD_GROUP_WIDTH

    def col_spec(width, col0):
        base = col0 // width
        return pl.BlockSpec((tb, width), lambda b, g, i: (b * nb + i, base + g))

    def w_spec(nrows, width, col0):
        base = col0 // width
        return pl.BlockSpec((None, nrows, width), lambda b, g, i: (layer, 0, base + g))

    seq_blk = pl.BlockSpec((tb, LANES), lambda b, g, i: (b * nb + i, 0))
    t_blk = pl.BlockSpec((None, SSD_HPG, tb), lambda b, g, i: (b * nb + i, g, 0))
    cd_blk = pl.BlockSpec((CHUNKS_PER_BLOCK, gw), lambda b, g, i: (b * nb + i, g))
    cx, cb_, cc = 0, SSD_WIDTH, SSD_WIDTH + SSD_GROUPS * SSD_STATE
    return pl.pallas_call(
        _ssd_kernel,
        out_shape=jax.ShapeDtypeStruct((t, SSD_WIDTH), BF16),
        grid=(batch, SSD_GROUPS, nb),
        in_specs=[col_spec(gw, COL_Z), col_spec(gw, COL_XS), col_spec(SSD_STATE, COL_B),
                  col_spec(SSD_STATE, COL_C),
                  seq_blk, cd_blk, t_blk, t_blk, t_blk,
                  w_spec(CONV_WIDTH, gw, cx), w_spec(CONV_WIDTH, SSD_STATE, cb_),
                  w_spec(CONV_WIDTH, SSD_STATE, cc),
                  w_spec(1, gw, cx), w_spec(1, SSD_STATE, cb_), w_spec(1, SSD_STATE, cc),
                  w_spec(1, gw, 0), w_spec(1, gw, 0)],
        out_specs=pl.BlockSpec((tb, gw), lambda b, g, i: (b * nb + i, g)),
        scratch_shapes=[pltpu.VMEM((tb + 2 * SUBLANES, gw), F32),
                        pltpu.VMEM((tb + 2 * SUBLANES, SSD_STATE), F32),
                        pltpu.VMEM((tb + 2 * SUBLANES, SSD_STATE), F32),
                        pltpu.VMEM((tb, gw), F32),
                        pltpu.VMEM((tb, SSD_STATE), F32),
                        pltpu.VMEM((tb, SSD_STATE), F32),
                        pltpu.VMEM((SSD_STATE, gw), F32)],
        compiler_params=_params(3),
        name="ssd_mixer",
    )(proj, proj, proj, proj, cum, cd, cum_t, pd_t, q1_t,
      conv_w, conv_w, conv_w, conv_b, conv_b, conv_b, d_exp, norm_w)


LRU_PER_STEP = LRU_BLOCKS // MLSTM_HEADS
LRU_STEP_WIDTH = LRU_PER_STEP * LRU_BLOCK_DIM


def _lru_gates(xc, wa, wx, ba, bx, sp_lam):
    xc16 = xc.astype(BF16)
    r = _sigmoid(_dot(xc16, wa) + ba)
    i = _sigmoid(_dot(xc16, wx) + bx)
    log_a = -LRU_C * r * sp_lam
    a = jnp.exp(log_a)
    var = -jnp.tanh(log_a) * (a * a + 1.0)
    u = jnp.where(var > 0.0, var * lax.rsqrt(var), 0.0) * (i * xc)
    return a, u


def _lru_scan(a, u, gate, keep, carry):
    tiles = CHUNK // SUBLANES
    a = a.reshape(tiles, SUBLANES, LANES)
    u = u.reshape(tiles, SUBLANES, LANES)
    d = 1
    while d < SUBLANES:
        a_sh = pltpu.roll(a, d, 1)
        u_sh = pltpu.roll(u, d, 1)
        u = jnp.where(keep[d], u + a * u_sh, u)
        a = jnp.where(keep[d], a * a_sh, a)
        d *= 2
    hs = []
    for t in range(tiles):
        hs.append(u[t] + a[t] * carry)
        carry = hs[-1][SUBLANES - 1:SUBLANES, :]
    return jnp.concatenate(hs, axis=0) * _silu(gate), carry


def _mlstm_lru_kernel(x_ref, o_ref, gate_ref, cum_ref, pd_ref, cum_t_ref, pd_t_ref, cw_ref, cb_ref,
                      wq_ref, wk_ref, wv_ref, nw_ref,
                      lx_ref, lgate_ref, lcw_ref, lcb_ref, wa_ref, wx_ref, ba_ref, bx_ref, lam_ref,
                      y_ref, yl_ref,
                      xp_scr, xc_scr, c_scr, n_scr, m_scr, lxp_scr, lxc_scr, lh_scr):
    h = pl.program_id(1)
    first = pl.program_id(2) == 0
    dh = MLSTM_HEAD_DIM

    @pl.when(first)
    def _():
        c_scr[...] = jnp.zeros(c_scr.shape, F32)
        n_scr[...] = jnp.zeros(n_scr.shape, F32)
        m_scr[...] = jnp.zeros(m_scr.shape, F32)
        lh_scr[...] = jnp.zeros(lh_scr.shape, F32)

    _conv_block(x_ref, xp_scr, cw_ref, cb_ref, xc_scr, first, dh, True)
    _conv_block(lx_ref, lxp_scr, lcw_ref, lcb_ref, lxc_scr, first, LRU_STEP_WIDTH, False)
    causal = _iota((CHUNK, CHUNK), 1) <= _iota((CHUNK, CHUNK), 0)
    is_head = _iota((MLSTM_HEADS, CHUNK), 0) == h
    c_st = c_scr[...]
    n_st = n_scr[...]
    m_prev = m_scr[...]
    row_in_tile = _iota((CHUNK // SUBLANES, SUBLANES, LANES), 1)
    keep = {d: row_in_tile >= d for d in (1, 2, 4)}
    sp_lam = _softplus(-lam_ref[...])
    lru_lanes = [slice(n * LRU_BLOCK_DIM, (n + 1) * LRU_BLOCK_DIM) for n in range(LRU_PER_STEP)]
    lru_carry = [lh_scr[:, lanes] for lanes in lru_lanes]

    for c in range(CHUNKS_PER_BLOCK):
        r0 = c * CHUNK

        def lru_gates(n):
            lanes = lru_lanes[n]
            return _lru_gates(lxc_scr[r0:r0 + CHUNK, lanes], wa_ref[n], wx_ref[n],
                              ba_ref[:, lanes], bx_ref[:, lanes], sp_lam[:, lanes])

        def lru_scan(n, a_u):
            lanes = lru_lanes[n]
            out_l, lru_carry[n] = _lru_scan(*a_u, lgate_ref[r0:r0 + CHUNK, lanes], keep, lru_carry[n])
            yl_ref[r0:r0 + CHUNK, lanes] = out_l.astype(BF16)

        xm16 = x_ref[r0:r0 + CHUNK, :].astype(BF16)
        xc16 = xc_scr[r0:r0 + CHUNK, :].astype(BF16)
        q = _dot(xc16, wq_ref[0])
        k = _dot(xc16, wk_ref[0]) * (dh ** -0.5)
        v = _dot(xm16, wv_ref[0])
        q16 = q.astype(BF16)
        v16 = v.astype(BF16)
        a_u0 = lru_gates(0)

        i_row = jnp.sum(jnp.where(is_head, pd_t_ref[:, r0:r0 + CHUNK], 0.0), axis=0, keepdims=True)
        b_row = jnp.sum(jnp.where(is_head, cum_t_ref[:, r0:r0 + CHUNK], 0.0), axis=0, keepdims=True)
        i_col = _column(pd_ref[r0:r0 + CHUNK, :], SMALL_I + h)
        b_col = _column(cum_ref[r0:r0 + CHUNK, :], SMALL_F + h)

        a_inter = b_col + m_prev
        d_intra = jnp.where(causal, b_col - b_row + i_row, -jnp.inf)
        m_t = jnp.maximum(a_inter, jnp.max(d_intra, axis=1, keepdims=True))
        w_intra = jnp.exp(d_intra - m_t)
        w_inter = jnp.exp(a_inter - m_t)
        lru_scan(0, a_u0)
        s = _dot_nt(q16, k.astype(BF16)) * w_intra
        num =_dot(s.astype(BF16), v16) + w_inter * _dot(q16, c_st.astype(BF16))
        den = jnp.sum(s, axis=1, keepdims=True) + w_inter * jnp.sum(q * n_st, axis=1, keepdims=True)
        hs = num * (1.0 / jnp.maximum(jnp.abs(den), jnp.exp(-m_t)))
        a_u1 = lru_gates(1)

        m_new = m_t[CHUNK - 1:CHUNK, :]
        b_last = b_col[CHUNK - 1:CHUNK, :]
        w_s = jnp.exp(b_last - b_col + i_col - m_new)
        carry_decay = jnp.exp(b_last + m_prev - m_new)
        kw = k * w_s
        c_st = carry_decay * c_st + _dot(kw.T.astype(BF16), v16)
        n_st = carry_decay * n_st + jnp.sum(kw, axis=0, keepdims=True)
        m_prev = m_new
        lru_scan(1, a_u1)

        hn = hs * lax.rsqrt(jnp.mean(hs * hs, axis=1, keepdims=True) + 1e-6) * nw_ref[...]
        out = hn * _sigmoid(o_ref[r0:r0 + CHUNK, :]) * _silu(gate_ref[r0:r0 + CHUNK, :])
        y_ref[r0:r0 + CHUNK, :] = out.astype(BF16)

    c_scr[...] = c_st
    n_scr[...] = n_st
    m_scr[...] = m_prev
    for n, lanes in enumerate(lru_lanes):
        lh_scr[:, lanes] = lru_carry[n]


def _mlstm_lru_mixer(proj, cum, pd, cum_t, pd_t, batch, seq, layer, conv_w, conv_b, w_q, w_k, w_v, norm_w,
                     lru_conv_w, lru_conv_b, lru_w_a, lru_w_x, lru_b_a, lru_b_x, lru_lam):
    t = proj.shape[0]
    nb = seq // TIME_BLOCK
    tb = TIME_BLOCK
    dh = MLSTM_HEAD_DIM
    lw = LRU_STEP_WIDTH
    assert lw == dh

    def col_spec(col0):
        base = col0 // dh
        return pl.BlockSpec((tb, dh), lambda b, h, i: (b * nb + i, base + h))

    def t_blk(row0):
        return pl.BlockSpec((None, MLSTM_HEADS, tb), lambda b, h, i: (b * nb + i, row0 // MLSTM_HEADS, 0))

    seq_blk = pl.BlockSpec((tb, LANES), lambda b, h, i: (b * nb + i, 0))
    vec = pl.BlockSpec((None, 1, dh), lambda b, h, i: (layer, 0, h))
    wspec = pl.BlockSpec((None, 1, dh, dh), lambda b, h, i: (layer, h, 0, 0))
    conv_spec = pl.BlockSpec((None, CONV_WIDTH, dh), lambda b, h, i: (layer, 0, h))
    lru_wspec = pl.BlockSpec((None, LRU_PER_STEP, LRU_BLOCK_DIM, LRU_BLOCK_DIM), lambda b, h, i: (layer, h, 0, 0))
    out_blk = pl.BlockSpec((tb, dh), lambda b, h, i: (b * nb + i, h))
    return pl.pallas_call(
        _mlstm_lru_kernel,
        out_shape=(jax.ShapeDtypeStruct((t, MLSTM_WIDTH), BF16), jax.ShapeDtypeStruct((t, LRU_WIDTH), BF16)),
        grid=(batch, MLSTM_HEADS, nb),
        in_specs=[col_spec(COL_ML_X), col_spec(COL_ML_O), col_spec(COL_ML_G),
                  seq_blk, seq_blk, t_blk(SMALL_F), t_blk(SMALL_I),
                  conv_spec, vec, wspec, wspec, wspec, vec,
                  col_spec(COL_LRU_X), col_spec(COL_LRU_G), conv_spec, vec,
                  lru_wspec, lru_wspec, vec, vec, vec],
        out_specs=(out_blk, out_blk),
        scratch_shapes=[pltpu.VMEM((tb + 2 * SUBLANES, dh), F32),
                        pltpu.VMEM((tb, dh), F32),
                        pltpu.VMEM((dh, dh), F32),
                        pltpu.VMEM((1, dh), F32),
                        pltpu.VMEM((1, 1), F32),
                        pltpu.VMEM((tb + 2 * SUBLANES, lw), F32),
                        pltpu.VMEM((tb, lw), F32),
                        pltpu.VMEM((1, lw), F32)],
        compiler_params=_params(3),
        name="mlstm_lru_mixer",
    )(proj, proj, proj, cum, pd, cum_t, pd_t, conv_w, conv_b, w_q, w_k, w_v, norm_w,
      proj, proj, lru_conv_w, lru_conv_b, lru_w_a, lru_w_x, lru_b_a, lru_b_x, lru_lam)


OUT_TM = 512
OUT_TK = 1024
OUT_K_SSD = SSD_WIDTH // OUT_TK
OUT_K_LRU = LRU_WIDTH // OUT_TK
OUT_K_ML = MLSTM_WIDTH // OUT_TK
OUT_K_STEPS = OUT_K_SSD + OUT_K_LRU + OUT_K_ML
OUT_X_ROWS = OUT_TM // OUT_K_STEPS
LN_ROWS = 64


def _out_ln_kernel(ys_ref, yl_ref, ym_ref, w_ref, x_ref, lnw_ref, lnb_ref, of_ref, ob_ref):
    k = pl.program_id(1)

    @pl.when(k == 0)
    def _():
        of_ref[...] = _dot(ys_ref[...], w_ref[...])

    @pl.when(jnp.logical_and(k > 0, k < OUT_K_SSD))
    def _():
        of_ref[...] += _dot(ys_ref[...], w_ref[...])

    @pl.when(jnp.logical_and(k >= OUT_K_SSD, k < OUT_K_SSD + OUT_K_LRU))
    def _():
        of_ref[...] += _dot(yl_ref[...], w_ref[...])

    @pl.when(k >= OUT_K_SSD + OUT_K_LRU)
    def _():
        of_ref[...] += _dot(ym_ref[...], w_ref[...])

    xr = pl.ds(pl.multiple_of(k * OUT_X_ROWS, OUT_X_ROWS), OUT_X_ROWS)
    of_ref[xr, :] += DEEPNORM_ALPHA * x_ref[...]

    @pl.when(k == OUT_K_STEPS - 1)
    def _():
        def ln(r, carry):
            r0 = pl.multiple_of(r * LN_ROWS, LN_ROWS)
            rows = pl.ds(r0, LN_ROWS)
            mu = jnp.mean(of_ref[rows, :], axis=1, keepdims=True)
            vc = of_ref[rows, :] - mu
            var = jnp.mean(vc * vc, axis=1, keepdims=True)
            o = (of_ref[rows, :] - mu) * lax.rsqrt(var + 1e-5) * lnw_ref[...] + lnb_ref[...]
            of_ref[pl.ds(r0, LN_ROWS), :] = o
            ob_ref[pl.ds(r0, LN_ROWS), :] = o.astype(BF16)
            return carry

        lax.fori_loop(0, OUT_TM // LN_ROWS, ln, 0, unroll=2)


def _out_ln(y_ssd, y_lru, y_ml, w_out16, x, layer, ln_w, ln_b):
    t = x.shape[0]
    tm, tk = OUT_TM, OUT_TK
    k1, k2 = OUT_K_SSD, OUT_K_SSD + OUT_K_LRU
    vec = pl.BlockSpec((None, 1, D_MODEL), lambda i, k: (layer, 0, 0))
    return pl.pallas_call(
        _out_ln_kernel,
        out_shape=(jax.ShapeDtypeStruct((t, D_MODEL), F32), jax.ShapeDtypeStruct((t, D_MODEL), BF16)),
        grid=(t // tm, OUT_K_STEPS),
        in_specs=[pl.BlockSpec((tm, tk), lambda i, k: (i, jnp.minimum(k, k1 - 1))),
                  pl.BlockSpec((tm, tk), lambda i, k: (i, jnp.clip(k - k1, 0, OUT_K_LRU - 1))),
                  pl.BlockSpec((tm, tk), lambda i, k: (i, jnp.clip(k - k2, 0, OUT_K_ML - 1))),
                  pl.BlockSpec((None, tk, D_MODEL), lambda i, k: (layer, k, 0)),
                  pl.BlockSpec((OUT_X_ROWS, D_MODEL), lambda i, k: (i * OUT_K_STEPS + k, 0)),
                  vec, vec],
        out_specs=(pl.BlockSpec((tm, D_MODEL), lambda i, k: (i, 0)),
                   pl.BlockSpec((tm, D_MODEL), lambda i, k: (i, 0))),
        compiler_params=_params(2),
        name="out_proj_ln",
    )(y_ssd, y_lru, y_ml, w_out16, x, ln_w, ln_b)


def _row3(p):
    return p.reshape(p.shape[0], 1, p.shape[1])


def kernel(x, w_in, ssd_conv_w, ssd_conv_b, ssd_dt_bias, ssd_a_log, ssd_d, ssd_norm_w, lru_conv_w, lru_conv_b,
           lru_w_a, lru_b_a, lru_w_x, lru_b_x, lru_lambda, mlstm_conv_w, mlstm_conv_b, mlstm_w_q, mlstm_w_k,
           mlstm_w_v, mlstm_b_i, mlstm_b_f, mlstm_norm_w, w_out, ln_w, ln_b):
    batch, seq, d = x.shape
    depth = w_in.shape[0]
    assert d == D_MODEL and seq % TIME_BLOCK == 0 and depth == DEPTH and w_in.shape[2] == N_IN
    xf = x.reshape(batch * seq, d)
    x16 = _cast_bf16(xf)

    wt = jnp.swapaxes(w_in, 1, 2)
    n_pad = LANES - SSD_HEADS - 2 * MLSTM_HEADS
    wt_small = jnp.concatenate([wt[:, N_HEAD:N_HEAD + SSD_HEADS], wt[:, N_IN - 2 * MLSTM_HEADS:],
                                jnp.zeros((depth, n_pad, d), F32)], axis=1)
    w_out16 = _cast_bf16(w_out.reshape(depth * MIX_WIDTH, d)).reshape(depth, MIX_WIDTH, d)
    small_bias = _row3(jnp.concatenate([ssd_dt_bias, mlstm_b_i, mlstm_b_f, jnp.zeros((depth, n_pad), F32)], axis=1))
    alog_row = _row3(jnp.concatenate([ssd_a_log, jnp.zeros((depth, LANES - SSD_HEADS), F32)], axis=1))
    d_exp = _row3(jnp.repeat(ssd_d, SSD_HEAD_DIM, axis=1))
    lru_wa16, lru_wx16 = lru_w_a.astype(BF16), lru_w_x.astype(BF16)
    wq16, wk16, wv16 = mlstm_w_q.astype(BF16), mlstm_w_k.astype(BF16), mlstm_w_v.astype(BF16)

    for l in range(depth):
        proj, small = _in_proj(x16, wt, wt_small, l)
        cum, pd, cd, cum_t, pd_t, q1_t = _head_prep(small, small_bias, alog_row, l)
        y_ssd = _ssd_mixer(proj, cum, cd, cum_t, pd_t, q1_t, batch, seq, l, ssd_conv_w, _row3(ssd_conv_b),
                           d_exp, _row3(ssd_norm_w))
        y_ml, y_lru = _mlstm_lru_mixer(
            proj, cum, pd, cum_t, pd_t, batch, seq, l, mlstm_conv_w, _row3(mlstm_conv_b),
            wq16, wk16, wv16, _row3(mlstm_norm_w),
            lru_conv_w, _row3(lru_conv_b), lru_wa16, lru_wx16, _row3(lru_b_a), _row3(lru_b_x), _row3(lru_lambda))
        xf, x16 = _out_ln(y_ssd, y_lru, y_ml, w_out16, xf, l, _row3(ln_w), _row3(ln_b))
    return xf.reshape(batch, seq, d)
```

```python
import jax
import jax.numpy as jnp
from jax import lax
from jax.experimental import pallas as pl
from jax.experimental.pallas import tpu as pltpu

F32 = jnp.float32
BF16 = jnp.bfloat16

D_MODEL = 4096
DEPTH = 2
MIX_WIDTH = 2 * D_MODEL
SSD_WIDTH = MIX_WIDTH // 2
LRU_WIDTH = MIX_WIDTH // 4
MLSTM_WIDTH = MIX_WIDTH - SSD_WIDTH - LRU_WIDTH
SSD_HEAD_DIM = 64
SSD_HEAD_SHIFT = 6
SSD_HEADS = SSD_WIDTH // SSD_HEAD_DIM
SSD_GROUPS = 8
SSD_HPG = SSD_HEADS // SSD_GROUPS
SSD_STATE = 128
SSD_GROUP_WIDTH = SSD_WIDTH // SSD_GROUPS
LRU_BLOCKS = 16
LRU_BLOCK_DIM = LRU_WIDTH // LRU_BLOCKS
LRU_C = 8.0
MLSTM_HEADS = 8
MLSTM_HEAD_DIM = MLSTM_WIDTH // MLSTM_HEADS
CONV_WIDTH = 4
CHUNK = 128
DEEPNORM_ALPHA = (2.0 * DEPTH) ** 0.25

SUBLANES = 8
LANES = 128

XBC_WIDTH = SSD_WIDTH + 2 * SSD_GROUPS * SSD_STATE
N_HEAD = SSD_WIDTH + XBC_WIDTH
N_TAIL = 2 * LRU_WIDTH + 3 * MLSTM_WIDTH
N_MAIN = N_HEAD + N_TAIL
N_IN = N_HEAD + SSD_HEADS + N_TAIL + 2 * MLSTM_HEADS
COL_Z = 0
COL_XS = SSD_WIDTH
COL_B = COL_XS + SSD_WIDTH
COL_C = COL_B + SSD_GROUPS * SSD_STATE
COL_LRU_X = COL_C + SSD_GROUPS * SSD_STATE
COL_LRU_G = COL_LRU_X + LRU_WIDTH
COL_ML_X = COL_LRU_G + LRU_WIDTH
COL_ML_O = COL_ML_X + MLSTM_WIDTH
COL_ML_G = COL_ML_O + MLSTM_WIDTH
SMALL_I = SSD_HEADS
SMALL_F = SSD_HEADS + MLSTM_HEADS

VMEM_LIMIT = 56 * 1024 * 1024
TIME_BLOCK = 2048
CHUNKS_PER_BLOCK = TIME_BLOCK // CHUNK


def _params(n_axes):
    return pltpu.CompilerParams(dimension_semantics=("arbitrary",) * n_axes, vmem_limit_bytes=VMEM_LIMIT)


def _sigmoid(x):
    return 0.5 * jnp.tanh(0.5 * x) + 0.5


def _silu(x):
    h = 0.5 * x
    return h + h * jnp.tanh(h)


def _softplus(x):
    return jnp.maximum(x, 0.0) + jnp.log1p(jnp.exp(-jnp.abs(x)))


def _dot(a, b):
    return jnp.dot(a, b, preferred_element_type=F32)


def _dot_nt(a, b):
    return lax.dot_general(a, b, (((1,), (1,)), ((), ())), preferred_element_type=F32)


def _dot_f32(a, b):
    return jnp.dot(a, b, precision=lax.Precision.HIGHEST, preferred_element_type=F32)


def _iota(shape, dim):
    return lax.broadcasted_iota(jnp.int32, shape, dim)


def _column(x, idx):
    lane = _iota(x.shape, 1)
    return jnp.sum(jnp.where(lane == idx, x, 0.0), axis=1, keepdims=True)


def _conv_stage(x_ref, xp_scr, first, width):
    tb = TIME_BLOCK

    @pl.when(first)
    def _():
        xp_scr[0:SUBLANES, :] = jnp.zeros((SUBLANES, width), F32)

    @pl.when(jnp.logical_not(first))
    def _():
        xp_scr[0:SUBLANES, :] = xp_scr[tb:tb + SUBLANES, :]

    xp_scr[SUBLANES:SUBLANES + tb, :] = x_ref[...]


def _conv_rows(xp_scr, w_ref, b_ref, r0, width, apply_silu):
    tiles = []
    for l0 in range(0, width, LANES):
        acc = b_ref[:, l0:l0 + LANES] + w_ref[3:4, l0:l0 + LANES] * xp_scr[r0 + 8:r0 + 8 + CHUNK, l0:l0 + LANES]
        for k in range(CONV_WIDTH - 1):
            off = r0 + 8 - (CONV_WIDTH - 1) + k
            acc = acc + w_ref[k:k + 1, l0:l0 + LANES] * xp_scr[off:off + CHUNK, l0:l0 + LANES]
        tiles.append(_silu(acc) if apply_silu else acc)
    return tiles[0] if len(tiles) == 1 else jnp.concatenate(tiles, axis=1)


def _conv_block(x_ref, xp_scr, w_ref, b_ref, out_scr, first, width, apply_silu):
    _conv_stage(x_ref, xp_scr, first, width)
    for c in range(CHUNKS_PER_BLOCK):
        r0 = c * CHUNK
        out_scr[r0:r0 + CHUNK, :] = _conv_rows(xp_scr, w_ref, b_ref, r0, width, apply_silu)


CAST_ROWS = 512


def _cast_kernel(a_ref, o_ref):
    o_ref[...] = a_ref[...].astype(BF16)


def _cast_bf16(a):
    r, c = a.shape
    return pl.pallas_call(
        _cast_kernel,
        out_shape=jax.ShapeDtypeStruct((r, c), BF16),
        grid=(r // CAST_ROWS,),
        in_specs=[pl.BlockSpec((CAST_ROWS, c), lambda i: (i, 0))],
        out_specs=pl.BlockSpec((CAST_ROWS, c), lambda i: (i, 0)),
        compiler_params=_params(1),
        name="cast_bf16",
    )(a)


IN_TM = 2048
IN_TN = 512
IN_SMALL_TM = 1024


def _in_proj_kernel(a_ref, wt_ref, ws_ref, o_ref, s_ref):
    o_ref[...] = _dot_nt(a_ref[...], wt_ref[...].astype(BF16))

    @pl.when(pl.program_id(1) == 0)
    def _():
        s_ref[...] = _dot_nt(a_ref[...], ws_ref[...].astype(BF16))


def _in_proj(x16, wt_all, wt_small, layer):
    m, k = x16.shape
    head_blocks = N_HEAD // IN_TN

    def w_index(i, j):
        row = j * IN_TN + jnp.where(j >= head_blocks, SSD_HEADS, 0)
        return (layer, pl.multiple_of(row, SSD_HEADS), 0)

    return pl.pallas_call(
        _in_proj_kernel,
        out_shape=(jax.ShapeDtypeStruct((m, N_MAIN), F32), jax.ShapeDtypeStruct((m, LANES), F32)),
        grid=(m // IN_TM, N_MAIN // IN_TN),
        in_specs=[pl.BlockSpec((IN_TM, k), lambda i, j: (i, 0), pipeline_mode=pl.Buffered(1)),
                  pl.BlockSpec((None, pl.Element(IN_TN), pl.Element(k)), w_index),
                  pl.BlockSpec((None, LANES, k), lambda i, j: (layer, 0, 0), pipeline_mode=pl.Buffered(1))],
        out_specs=(pl.BlockSpec((IN_TM, IN_TN), lambda i, j: (i, j)),
                   pl.BlockSpec((IN_TM, LANES), lambda i, j: (i, 0))),
        compiler_params=_params(2),
        name="in_proj",
    )(x16, wt_all, wt_small)


def _head_prep_kernel(sm_ref, bias_ref, alog_ref, cum_ref, pd_ref, cd_ref, cum_t_ref, pd_t_ref, q1_t_ref):
    ltri = (_iota((CHUNK, CHUNK), 1) <= _iota((CHUNK, CHUNK), 0)).astype(F32)
    lane = _iota((1, LANES), 1)
    is_ssd = lane < SSD_HEADS
    a_neg = jnp.where(is_ssd, -jnp.exp(alog_ref[...]), 0.0)
    chunk_decay = []
    for c in range(CHUNKS_PER_BLOCK):
        r0 = c * CHUNK
        pre = sm_ref[r0:r0 + CHUNK, :] + bias_ref[...]
        dt = _softplus(pre)
        log_f = -_softplus(-pre)
        cum = _dot_f32(ltri, jnp.where(is_ssd, dt * a_neg, log_f))
        pd = jnp.where(is_ssd, dt, pre)
        cum_ref[r0:r0 + CHUNK, :] = cum
        pd_ref[r0:r0 + CHUNK, :] = pd
        cum_last = cum[CHUNK - 1:CHUNK, :]
        cum_t_ref[:, r0:r0 + CHUNK] = cum.T
        pd_t_ref[:, r0:r0 + CHUNK] = pd.T
        q1_t_ref[:, r0:r0 + CHUNK] = (jnp.exp(cum_last - cum) * dt).T
        chunk_decay.append(jnp.exp(cum_last))
    cd = jnp.concatenate(chunk_decay, axis=0)
    cd_hi = cd.astype(BF16)
    cd_lo = (cd - cd_hi.astype(F32)).astype(BF16)
    expand = (jnp.right_shift(_iota((LANES, SSD_WIDTH), 1), SSD_HEAD_SHIFT)
              == _iota((LANES, SSD_WIDTH), 0)).astype(F32).astype(BF16)
    cd_ref[...] = _dot(cd_hi, expand) + _dot(cd_lo, expand)


def _head_prep(small, bias_all, alog_all, layer):
    t = small.shape[0]
    tb = TIME_BLOCK
    nb = t // tb
    row_blk = pl.BlockSpec((tb, LANES), lambda i: (i, 0))
    vec = pl.BlockSpec((None, 1, LANES), lambda i: (layer, 0, 0))
    t_blk = pl.BlockSpec((None, LANES, tb), lambda i: (i, 0, 0))
    flat = jax.ShapeDtypeStruct((t, LANES), F32)
    transposed = jax.ShapeDtypeStruct((nb, LANES, tb), F32)
    cd_shape = jax.ShapeDtypeStruct((t // CHUNK, SSD_WIDTH), F32)
    cd_blk = pl.BlockSpec((CHUNKS_PER_BLOCK, SSD_WIDTH), lambda i: (i, 0))
    return pl.pallas_call(
        _head_prep_kernel,
        out_shape=(flat, flat, cd_shape, transposed, transposed, transposed),
        grid=(nb,),
        in_specs=[row_blk, vec, vec],
        out_specs=(row_blk, row_blk, cd_blk, t_blk, t_blk, t_blk),
        compiler_params=_params(1),
        name="head_prep",
    )(small, bias_all, alog_all)


def _ssd_kernel(z_ref, xs_ref, bm_ref, cm_ref, cum_ref, cd_ref, cum_t_ref, dt_t_ref, q1_t_ref,
                cwx_ref, cwb_ref, cwc_ref, cbx_ref, cbb_ref, cbc_ref, dexp_ref, nw_ref,
                y_ref,
                xpx_scr, xpb_scr, xpc_scr, st_scr):
    g = pl.program_id(1)
    first = pl.program_id(2) == 0
    gw = SSD_GROUP_WIDTH

    @pl.when(first)
    def _():
        st_scr[...] = jnp.zeros(st_scr.shape, F32)

    _conv_stage(xs_ref, xpx_scr, first, gw)
    _conv_stage(bm_ref, xpb_scr, first, SSD_STATE)
    _conv_stage(cm_ref, xpc_scr, first, SSD_STATE)

    causal = _iota((CHUNK, CHUNK), 1) <= _iota((CHUNK, CHUNK), 0)
    low_half = _iota((CHUNK, LANES), 1) < SSD_HEAD_DIM
    prev = st_scr[...]

    for c in range(CHUNKS_PER_BLOCK):
        r0 = c * CHUNK
        xs = _conv_rows(xpx_scr, cwx_ref, cbx_ref, r0, gw, True)
        bm = _conv_rows(xpb_scr, cwb_ref, cbb_ref, r0, SSD_STATE, True)
        cm16 = _conv_rows(xpc_scr, cwc_ref, cbc_ref, r0, SSD_STATE, True).astype(BF16)
        acum_all = cum_ref[r0:r0 + CHUNK, :]
        acum_t = cum_t_ref[:, r0:r0 + CHUNK]
        dt_t = dt_t_ref[:, r0:r0 + CHUNK]
        q1_t = q1_t_ref[:, r0:r0 + CHUNK]

        cb = _dot_nt(cm16, bm.astype(BF16))
        bm_t = bm.T
        c_prev =_dot(cm16, prev.astype(BF16))

        ys, sts = [], []
        for j in range(SSD_HPG // 2):
            pair = slice(j * LANES, (j + 1) * LANES)
            wts, bts, q2 = [], [], []
            for r in (2 * j, 2 * j + 1):
                acol = _column(acum_all, g * SSD_HPG + r)
                seg = acol - acum_t[r:r + 1, :]
                decay = jnp.exp(jnp.where(causal, seg, -jnp.inf))
                wts.append((cb * decay * dt_t[r:r + 1, :]).astype(BF16))
                bts.append((bm_t * q1_t[r:r + 1, :]).astype(BF16))
                q2.append(jnp.exp(acol))
            xp = xs[:, pair]
            rhs = jnp.concatenate([jnp.where(low_half, xp, 0.0).astype(BF16),
                                   jnp.where(low_half, 0.0, xp).astype(BF16)], axis=0)
            y_diag = _dot(jnp.concatenate(wts, axis=1), rhs)
            sts.append(_dot(jnp.concatenate(bts, axis=1), rhs))
            ys.append(y_diag + c_prev[:, pair] * jnp.where(low_half, q2[0], q2[1]))
        y = jnp.concatenate(ys, axis=1)
        prev = prev * cd_ref[c:c + 1, :] + jnp.concatenate(sts, axis=1)

        y = y + dexp_ref[...] * xs
        y = y * _silu(z_ref[r0:r0 + CHUNK, :])
        y = y * lax.rsqrt(jnp.mean(y * y, axis=1, keepdims=True) + 1e-6)
        y_ref[r0:r0 + CHUNK, :] = (y * nw_ref[...]).astype(BF16)

    st_scr[...] = prev


def _ssd_mixer(proj, cum, cd, cum_t, pd_t, q1_t, batch, seq, layer, conv_w, conv_b, d_exp, norm_w):
    t = proj.shape[0]
    nb = seq // TIME_BLOCK
    tb = TIME_BLOCK
    gw = SSD_GROUP_WIDTH

    def col_spec(width, col0):
        base = col0 // width
        return pl.BlockSpec((tb, width), lambda b, g, i: (b * nb + i, base + g))

    def w_spec(nrows, width, col0):
        base = col0 // width
        return pl.BlockSpec((None, nrows, width), lambda b, g, i: (layer, 0, base + g))

    seq_blk = pl.BlockSpec((tb, LANES), lambda b, g, i: (b * nb + i, 0))
    t_blk = pl.BlockSpec((None, SSD_HPG, tb), lambda b, g, i: (b * nb + i, g, 0))
    cd_blk = pl.BlockSpec((CHUNKS_PER_BLOCK, gw), lambda b, g, i: (b * nb + i, g))
    cx, cb_, cc = 0, SSD_WIDTH, SSD_WIDTH + SSD_GROUPS * SSD_STATE
    return pl.pallas_call(
        _ssd_kernel,
        out_shape=jax.ShapeDtypeStruct((t, SSD_WIDTH), BF16),
        grid=(batch, SSD_GROUPS, nb),
        in_specs=[col_spec(gw, COL_Z), col_spec(gw, COL_XS), col_spec(SSD_STATE, COL_B),
                  col_spec(SSD_STATE, COL_C),
                  seq_blk, cd_blk, t_blk, t_blk, t_blk,
                  w_spec(CONV_WIDTH, gw, cx), w_spec(CONV_WIDTH, SSD_STATE, cb_),
                  w_spec(CONV_WIDTH, SSD_STATE, cc),
                  w_spec(1, gw, cx), w_spec(1, SSD_STATE, cb_), w_spec(1, SSD_STATE, cc),
                  w_spec(1, gw, 0), w_spec(1, gw, 0)],
        out_specs=pl.BlockSpec((tb, gw), lambda b, g, i: (b * nb + i, g)),
        scratch_shapes=[pltpu.VMEM((tb + 2 * SUBLANES, gw), F32),
                        pltpu.VMEM((tb + 2 * SUBLANES, SSD_STATE), F32),
                        pltpu.VMEM((tb + 2 * SUBLANES, SSD_STATE), F32),
                        pltpu.VMEM((SSD_STATE, gw), F32)],
        compiler_params=_params(3),
        name="ssd_mixer",
    )(proj, proj, proj, proj, cum, cd, cum_t, pd_t, q1_t,
      conv_w, conv_w, conv_w, conv_b, conv_b, conv_b, d_exp, norm_w)


LRU_PER_STEP = LRU_BLOCKS // MLSTM_HEADS
LRU_STEP_WIDTH = LRU_PER_STEP * LRU_BLOCK_DIM


def _lru_gates(xc, wa, wx, ba, bx, sp_lam):
    xc16 = xc.astype(BF16)
    r = _sigmoid(_dot(xc16, wa) + ba)
    i = _sigmoid(_dot(xc16, wx) + bx)
    log_a = -LRU_C * r * sp_lam
    a = jnp.exp(log_a)
    var = -jnp.tanh(log_a) * (a * a + 1.0)
    u = jnp.where(var > 0.0, var * lax.rsqrt(var), 0.0) * (i * xc)
    return a, u


def _lru_scan(a, u, gate, keep, carry):
    tiles = CHUNK // SUBLANES
    a = a.reshape(tiles, SUBLANES, LANES)
    u = u.reshape(tiles, SUBLANES, LANES)
    d = 1
    while d < SUBLANES:
        a_sh = pltpu.roll(a, d, 1)
        u_sh = pltpu.roll(u, d, 1)
        u = jnp.where(keep[d], u + a * u_sh, u)
        a = jnp.where(keep[d], a * a_sh, a)
        d *= 2
    hs = []
    for t in range(tiles):
        hs.append(u[t] + a[t] * carry)
        carry = hs[-1][SUBLANES - 1:SUBLANES, :]
    return jnp.concatenate(hs, axis=0) * _silu(gate), carry


def _mlstm_lru_kernel(x_ref, o_ref, gate_ref, cum_ref, pd_ref, cum_t_ref, pd_t_ref, cw_ref, cb_ref,
                      wq_ref, wk_ref, wv_ref, nw_ref,
                      lx_ref, lgate_ref, lcw_ref, lcb_ref, wa_ref, wx_ref, ba_ref, bx_ref, lam_ref,
                      y_ref, yl_ref,
                      xp_scr, xc_scr, c_scr, n_scr, m_scr, lxp_scr, lxc_scr, lh_scr):
    h = pl.program_id(1)
    first = pl.program_id(2) == 0
    dh = MLSTM_HEAD_DIM

    @pl.when(first)
    def _():
        c_scr[...] = jnp.zeros(c_scr.shape, F32)
        n_scr[...] = jnp.zeros(n_scr.shape, F32)
        m_scr[...] = jnp.zeros(m_scr.shape, F32)
        lh_scr[...] = jnp.zeros(lh_scr.shape, F32)

    _conv_block(x_ref, xp_scr, cw_ref, cb_ref, xc_scr, first, dh, True)
    _conv_block(lx_ref, lxp_scr, lcw_ref, lcb_ref, lxc_scr, first, LRU_STEP_WIDTH, False)
    causal = _iota((CHUNK, CHUNK), 1) <= _iota((CHUNK, CHUNK), 0)
    is_head = _iota((MLSTM_HEADS, CHUNK), 0) == h
    c_st = c_scr[...]
    n_st = n_scr[...]
    m_prev = m_scr[...]
    row_in_tile = _iota((CHUNK // SUBLANES, SUBLANES, LANES), 1)
    keep = {d: row_in_tile >= d for d in (1, 2, 4)}
    sp_lam = _softplus(-lam_ref[...])
    lru_lanes = [slice(n * LRU_BLOCK_DIM, (n + 1) * LRU_BLOCK_DIM) for n in range(LRU_PER_STEP)]
    lru_carry = [lh_scr[:, lanes] for lanes in lru_lanes]

    for c in range(CHUNKS_PER_BLOCK):
        r0 = c * CHUNK

        def lru_gates(n):
            lanes = lru_lanes[n]
            return _lru_gates(lxc_scr[r0:r0 + CHUNK, lanes], wa_ref[n], wx_ref[n],
                              ba_ref[:, lanes], bx_ref[:, lanes], sp_lam[:, lanes])

        def lru_scan(n, a_u):
            lanes = lru_lanes[n]
            out_l, lru_carry[n] = _lru_scan(*a_u, lgate_ref[r0:r0 + CHUNK, lanes], keep, lru_carry[n])
            yl_ref[r0:r0 + CHUNK, lanes] = out_l.astype(BF16)

        xm16 = x_ref[r0:r0 + CHUNK, :].astype(BF16)
        xc16 = xc_scr[r0:r0 + CHUNK, :].astype(BF16)
        q = _dot(xc16, wq_ref[0])
        k = _dot(xc16, wk_ref[0]) * (dh ** -0.5)
        v = _dot(xm16, wv_ref[0])
        q16 = q.astype(BF16)
        v16 = v.astype(BF16)
        a_u0 = lru_gates(0)

        i_row = jnp.sum(jnp.where(is_head, pd_t_ref[:, r0:r0 + CHUNK], 0.0), axis=0, keepdims=True)
        b_row = jnp.sum(jnp.where(is_head, cum_t_ref[:, r0:r0 + CHUNK], 0.0), axis=0, keepdims=True)
        i_col = _column(pd_ref[r0:r0 + CHUNK, :], SMALL_I + h)
        b_col = _column(cum_ref[r0:r0 + CHUNK, :], SMALL_F + h)

        a_inter = b_col + m_prev
        d_intra = jnp.where(causal, b_col - b_row + i_row, -jnp.inf)
        m_t = jnp.maximum(a_inter, jnp.max(d_intra, axis=1, keepdims=True))
        w_intra = jnp.exp(d_intra - m_t)
        w_inter = jnp.exp(a_inter - m_t)
        lru_scan(0, a_u0)
        s = _dot_nt(q16, k.astype(BF16)) * w_intra
        num =_dot(s.astype(BF16), v16) + w_inter * _dot(q16, c_st.astype(BF16))
        den = jnp.sum(s, axis=1, keepdims=True) + w_inter * jnp.sum(q * n_st, axis=1, keepdims=True)
        hs = num * (1.0 / jnp.maximum(jnp.abs(den), jnp.exp(-m_t)))
        a_u1 = lru_gates(1)

        m_new = m_t[CHUNK - 1:CHUNK, :]
        b_last = b_col[CHUNK - 1:CHUNK, :]
        w_s = jnp.exp(b_last - b_col + i_col - m_new)
        carry_decay = jnp.exp(b_last + m_prev - m_new)
        kw = k * w_s
        c_st = carry_decay * c_st + _dot(kw.T.astype(BF16), v16)
        n_st = carry_decay * n_st + jnp.sum(kw, axis=0, keepdims=True)
        m_prev = m_new
        lru_scan(1, a_u1)

        hn = hs * lax.rsqrt(jnp.mean(hs * hs, axis=1, keepdims=True) + 1e-6) * nw_ref[...]
        out = hn * _sigmoid(o_ref[r0:r0 + CHUNK, :]) * _silu(gate_ref[r0:r0 + CHUNK, :])
        y_ref[r0:r0 + CHUNK, :] = out.astype(BF16)

    c_scr[...] = c_st
    n_scr[...] = n_st
    m_scr[...] = m_prev
    for n, lanes in enumerate(lru_lanes):
        lh_scr[:, lanes] = lru_carry[n]


def _mlstm_lru_mixer(proj, cum, pd, cum_t, pd_t, batch, seq, layer, conv_w, conv_b, w_q, w_k, w_v, norm_w,
                     lru_conv_w, lru_conv_b, lru_w_a, lru_w_x, lru_b_a, lru_b_x, lru_lam):
    t = proj.shape[0]
    nb = seq // TIME_BLOCK
    tb = TIME_BLOCK
    dh = MLSTM_HEAD_DIM
    lw = LRU_STEP_WIDTH
    assert lw == dh

    def col_spec(col0):
        base = col0 // dh
        return pl.BlockSpec((tb, dh), lambda b, h, i: (b * nb + i, base + h))

    def t_blk(row0):
        return pl.BlockSpec((None, MLSTM_HEADS, tb), lambda b, h, i: (b * nb + i, row0 // MLSTM_HEADS, 0))

    seq_blk = pl.BlockSpec((tb, LANES), lambda b, h, i: (b * nb + i, 0))
    vec = pl.BlockSpec((None, 1, dh), lambda b, h, i: (layer, 0, h))
    wspec = pl.BlockSpec((None, 1, dh, dh), lambda b, h, i: (layer, h, 0, 0))
    conv_spec = pl.BlockSpec((None, CONV_WIDTH, dh), lambda b, h, i: (layer, 0, h))
    lru_wspec = pl.BlockSpec((None, LRU_PER_STEP, LRU_BLOCK_DIM, LRU_BLOCK_DIM), lambda b, h, i: (layer, h, 0, 0))
    out_blk = pl.BlockSpec((tb, dh), lambda b, h, i: (b * nb + i, h))
    return pl.pallas_call(
        _mlstm_lru_kernel,
        out_shape=(jax.ShapeDtypeStruct((t, MLSTM_WIDTH), BF16), jax.ShapeDtypeStruct((t, LRU_WIDTH), BF16)),
        grid=(batch, MLSTM_HEADS, nb),
        in_specs=[col_spec(COL_ML_X), col_spec(COL_ML_O), col_spec(COL_ML_G),
                  seq_blk, seq_blk, t_blk(SMALL_F), t_blk(SMALL_I),
                  conv_spec, vec, wspec, wspec, wspec, vec,
                  col_spec(COL_LRU_X), col_spec(COL_LRU_G), conv_spec, vec,
                  lru_wspec, lru_wspec, vec, vec, vec],
        out_specs=(out_blk, out_blk),
        scratch_shapes=[pltpu.VMEM((tb + 2 * SUBLANES, dh), F32),
                        pltpu.VMEM((tb, dh), F32),
                        pltpu.VMEM((dh, dh), F32),
                        pltpu.VMEM((1, dh), F32),
                        pltpu.VMEM((1, 1), F32),
                        pltpu.VMEM((tb + 2 * SUBLANES, lw), F32),
                        pltpu.VMEM((tb, lw), F32),
                        pltpu.VMEM((1, lw), F32)],
        compiler_params=_params(3),
        name="mlstm_lru_mixer",
    )(proj, proj, proj, cum, pd, cum_t, pd_t, conv_w, conv_b, w_q, w_k, w_v, norm_w,
      proj, proj, lru_conv_w, lru_conv_b, lru_w_a, lru_w_x, lru_b_a, lru_b_x, lru_lam)


OUT_TM = 512
OUT_TK = 1024
OUT_K_SSD = SSD_WIDTH // OUT_TK
OUT_K_LRU = LRU_WIDTH // OUT_TK
OUT_K_ML = MLSTM_WIDTH // OUT_TK
OUT_K_STEPS = OUT_K_SSD + OUT_K_LRU + OUT_K_ML
OUT_X_ROWS = OUT_TM // OUT_K_STEPS
LN_ROWS = 128


def _out_ln_kernel(ys_ref, yl_ref, ym_ref, w_ref, x_ref, lnw_ref, lnb_ref, of_ref, ob_ref):
    k = pl.program_id(1)

    @pl.when(k == 0)
    def _():
        of_ref[...] = _dot(ys_ref[...], w_ref[...])

    @pl.when(jnp.logical_and(k > 0, k < OUT_K_SSD))
    def _():
        of_ref[...] += _dot(ys_ref[...], w_ref[...])

    @pl.when(jnp.logical_and(k >= OUT_K_SSD, k < OUT_K_SSD + OUT_K_LRU))
    def _():
        of_ref[...] += _dot(yl_ref[...], w_ref[...])

    @pl.when(k >= OUT_K_SSD + OUT_K_LRU)
    def _():
        of_ref[...] += _dot(ym_ref[...], w_ref[...])

    xr = pl.ds(pl.multiple_of(k * OUT_X_ROWS, OUT_X_ROWS), OUT_X_ROWS)
    of_ref[xr, :] += DEEPNORM_ALPHA * x_ref[...]

    @pl.when(k == OUT_K_STEPS - 1)
    def _():
        def ln(r, carry):
            r0 = pl.multiple_of(r * LN_ROWS, LN_ROWS)
            rows = pl.ds(r0, LN_ROWS)
            mu = jnp.mean(of_ref[rows, :], axis=1, keepdims=True)
            vc = of_ref[rows, :] - mu
            var = jnp.mean(vc * vc, axis=1, keepdims=True)
            o = (of_ref[rows, :] - mu) * lax.rsqrt(var + 1e-5) * lnw_ref[...] + lnb_ref[...]
            of_ref[pl.ds(r0, LN_ROWS), :] = o
            ob_ref[pl.ds(r0, LN_ROWS), :] = o.astype(BF16)
            return carry

        lax.fori_loop(0, OUT_TM // LN_ROWS, ln, 0, unroll=2)


def _out_ln(y_ssd, y_lru, y_ml, w_out16, x, layer, ln_w, ln_b):
    t = x.shape[0]
    tm, tk = OUT_TM, OUT_TK
    k1, k2 = OUT_K_SSD, OUT_K_SSD + OUT_K_LRU
    vec = pl.BlockSpec((None, 1, D_MODEL), lambda i, k: (layer, 0, 0))
    return pl.pallas_call(
        _out_ln_kernel,
        out_shape=(jax.ShapeDtypeStruct((t, D_MODEL), F32), jax.ShapeDtypeStruct((t, D_MODEL), BF16)),
        grid=(t // tm, OUT_K_STEPS),
        in_specs=[pl.BlockSpec((tm, tk), lambda i, k: (i, jnp.minimum(k, k1 - 1))),
                  pl.BlockSpec((tm, tk), lambda i, k: (i, jnp.clip(k - k1, 0, OUT_K_LRU - 1))),
                  pl.BlockSpec((tm, tk), lambda i, k: (i, jnp.clip(k - k2, 0, OUT_K_ML - 1))),
                  pl.BlockSpec((None, tk, D_MODEL), lambda i, k: (layer, k, 0)),
                  pl.BlockSpec((OUT_X_ROWS, D_MODEL), lambda i, k: (i * OUT_K_STEPS + k, 0)),
                  vec, vec],
        out_specs=(pl.BlockSpec((tm, D_MODEL), lambda i, k: (i, 0)),
                   pl.BlockSpec((tm, D_MODEL), lambda i, k: (i, 0))),
        compiler_params=_params(2),
        name="out_proj_ln",
    )(y_ssd, y_lru, y_ml, w_out16, x, ln_w, ln_b)


def _row3(p):
    return p.reshape(p.shape[0], 1, p.shape[1])


def kernel(x, w_in, ssd_conv_w, ssd_conv_b, ssd_dt_bias, ssd_a_log, ssd_d, ssd_norm_w, lru_conv_w, lru_conv_b,
           lru_w_a, lru_b_a, lru_w_x, lru_b_x, lru_lambda, mlstm_conv_w, mlstm_conv_b, mlstm_w_q, mlstm_w_k,
           mlstm_w_v, mlstm_b_i, mlstm_b_f, mlstm_norm_w, w_out, ln_w, ln_b):
    batch, seq, d = x.shape
    depth = w_in.shape[0]
    assert d == D_MODEL and seq % TIME_BLOCK == 0 and depth == DEPTH and w_in.shape[2] == N_IN
    xf = x.reshape(batch * seq, d)
    x16 = _cast_bf16(xf)

    wt = jnp.swapaxes(w_in, 1, 2)
    n_pad = LANES - SSD_HEADS - 2 * MLSTM_HEADS
    wt_small = jnp.concatenate([wt[:, N_HEAD:N_HEAD + SSD_HEADS], wt[:, N_IN - 2 * MLSTM_HEADS:],
                                jnp.zeros((depth, n_pad, d), F32)], axis=1)
    w_out16 = _cast_bf16(w_out.reshape(depth * MIX_WIDTH, d)).reshape(depth, MIX_WIDTH, d)
    small_bias = _row3(jnp.concatenate([ssd_dt_bias, mlstm_b_i, mlstm_b_f, jnp.zeros((depth, n_pad), F32)], axis=1))
    alog_row = _row3(jnp.concatenate([ssd_a_log, jnp.zeros((depth, LANES - SSD_HEADS), F32)], axis=1))
    d_exp = _row3(jnp.repeat(ssd_d, SSD_HEAD_DIM, axis=1))
    lru_wa16, lru_wx16 = lru_w_a.astype(BF16), lru_w_x.astype(BF16)
    wq16, wk16, wv16 = mlstm_w_q.astype(BF16), mlstm_w_k.astype(BF16), mlstm_w_v.astype(BF16)

    for l in range(depth):
        proj, small = _in_proj(x16, wt, wt_small, l)
        cum, pd, cd, cum_t, pd_t, q1_t = _head_prep(small, small_bias, alog_row, l)
        y_ssd = _ssd_mixer(proj, cum, cd, cum_t, pd_t, q1_t, batch, seq, l, ssd_conv_w, _row3(ssd_conv_b),
                           d_exp, _row3(ssd_norm_w))
        y_ml, y_lru = _mlstm_lru_mixer(
            proj, cum, pd, cum_t, pd_t, batch, seq, l, mlstm_conv_w, _row3(mlstm_conv_b),
            wq16, wk16, wv16, _row3(mlstm_norm_w),
            lru_conv_w, _row3(lru_conv_b), lru_wa16, lru_wx16, _row3(lru_b_a), _row3(lru_b_x), _row3(lru_lambda))
        xf, x16 = _out_ln(y_ssd, y_lru, y_ml, w_out16, xf, l, _row3(ln_w), _row3(ln_b))
    return xf.reshape(batch, seq, d)
```
